```python
import jax, jax.numpy as jnp
from jax import lax
import numpy as np

D_MODEL = 1024
BATCH = 8
SEQ = 4096
DEPTH = 2

HEAD_DIM = 64
ATT_HEADS = D_MODEL // 128
ATT_WIDTH = ATT_HEADS * HEAD_DIM
DILATED_BRANCHES = ((128, 1), (512, 4), (2048, 16))
SSD_HEAD_DIM = 64
SSD_HEADS = D_MODEL // 128
SSD_WIDTH = SSD_HEADS * SSD_HEAD_DIM
SSD_GROUPS = 2
SSD_STATE = 128
SSD_CONV = 5
SSD_CHUNK = 128
SSD_CONV_DIM = SSD_WIDTH + 2 * SSD_GROUPS * SSD_STATE
CONV_GROUPS = 8
CONV_WIDTH = D_MODEL // 2
SHORT_CONV = 3
D_MIX = ATT_WIDTH + SSD_WIDTH + CONV_WIDTH
SPLIT_SIZES = (ATT_WIDTH, ATT_WIDTH, ATT_WIDTH,
               SSD_WIDTH, SSD_CONV_DIM, 2 * SSD_HEADS,
               CONV_WIDTH, CONV_WIDTH, CONV_WIDTH)
SPLIT_POINTS = tuple(int(v) for v in np.cumsum(SPLIT_SIZES)[:-1])
D_IN = int(sum(SPLIT_SIZES))
D_FF = 2816
FFN_CONV = 3
EPS = 1e-6
NEG = -1e30

kernel_name = "hybrid_dilated_ssd_shortconv_encoder"


def rmsnorm(x, g):
    xf = x.astype(jnp.float32)
    y = xf * lax.rsqrt(jnp.mean(xf * xf, axis=-1, keepdims=True) + EPS)
    return (y * g.astype(jnp.float32)).astype(x.dtype)


def group_rmsnorm(y, g, n_groups, out_dtype):
    shp = y.shape
    yf = y.astype(jnp.float32).reshape(*shp[:-1], n_groups, shp[-1] // n_groups)
    yf = yf * lax.rsqrt(jnp.mean(yf * yf, axis=-1, keepdims=True) + EPS)
    return (yf.reshape(shp) * g.astype(jnp.float32)).astype(out_dtype)


def dwconv_centred(x, w, b):
    k = w.shape[0]
    y = lax.conv_general_dilated(
        x, w[:, None, :].astype(x.dtype), window_strides=(1,),
        padding=[(k // 2, k // 2)], dimension_numbers=('NWC', 'WIO', 'NWC'),
        feature_group_count=x.shape[-1])
    return y + b.astype(x.dtype)


def alibi_slopes(n):
    return 2.0 ** (-8.0 * (jnp.arange(n, dtype=jnp.float32) + 1.0) / n)


def dilated_branch(q, k, v, slopes, window, dilation):
    b, s, h, dh = q.shape
    half = window // (2 * dilation)
    blk = half
    l = s // dilation
    nb = -(-l // blk)
    lp = nb * blk

    def to_sub(t):
        t = t.reshape(b, l, dilation, h, dh).transpose(0, 2, 3, 1, 4)
        return jnp.pad(t, ((0, 0), (0, 0), (0, 0), (0, lp - l), (0, 0)))

    def key_blocks(t):
        t = jnp.pad(to_sub(t), ((0, 0), (0, 0), (0, 0), (blk, blk), (0, 0)))
        t = t.reshape(b, dilation, h, nb + 2, blk, dh)
        return jnp.concatenate([t[:, :, :, :-2], t[:, :, :, 1:-1], t[:, :, :, 2:]], axis=4)

    qs = to_sub(q).reshape(b, dilation, h, nb, blk, dh)
    kb = key_blocks(k)
    vb = key_blocks(v)
    qi = jnp.arange(blk)[:, None]
    kj = jnp.arange(3 * blk)[None, :]
    rel = kj - blk - qi
    kidx = jnp.arange(nb)[:, None, None] * blk + kj[None] - blk
    valid = (jnp.abs(rel) <= half)[None] & (kidx >= 0) & (kidx < l)
    bias = -slopes[:, None, None, None] * (dilation * jnp.abs(rel)).astype(jnp.float32)
    scores = jnp.einsum('brhnqd,brhnkd->brhnqk', qs, kb) * (dh ** -0.5) + bias
    scores = jnp.where(valid, scores, NEG)
    lse = jax.nn.logsumexp(scores, axis=-1)
    o = jnp.einsum('brhnqk,brhnkd->brhnqd', jnp.exp(scores - lse[..., None]), vb)
    o = o.reshape(b, dilation, h, lp, dh)[:, :, :, :l].transpose(0, 3, 1, 2, 4).reshape(b, s, h, dh)
    lse = lse.reshape(b, dilation, h, lp)[:, :, :, :l].transpose(0, 3, 1, 2).reshape(b, s, h)
    return o, lse


def dilated_attention(q, k, v):
    b, s, _ = q.shape
    q, k, v = (t.astype(jnp.float32).reshape(b, s, ATT_HEADS, HEAD_DIM) for t in (q, k, v))
    slopes = alibi_slopes(ATT_HEADS)
    outs, lses = [], []
    for window, dilation in DILATED_BRANCHES:
        o, lse = dilated_branch(q, k, v, slopes, window, dilation)
        outs.append(o)
        lses.append(lse)
    wts = jax.nn.softmax(jnp.stack(lses, axis=0), axis=0)
    out = jnp.einsum('ibsh,ibshd->bshd', wts, jnp.stack(outs, axis=0))
    return out.reshape(b, s, ATT_WIDTH)


def ssd_scan(x, dt, a, bm, cm):
    b, s, h, p = x.shape
    g, n = bm.shape[2], bm.shape[3]
    e = h // g
    l = SSD_CHUNK
    c = s // l
    xc = (x * dt[..., None]).reshape(b, c, l, g, e, p)
    ac = (dt * a).reshape(b, c, l, g, e).transpose(0, 3, 4, 1, 2)
    bc = bm.reshape(b, c, l, g, n)
    cc = cm.reshape(b, c, l, g, n)
    acum = jnp.cumsum(ac, axis=-1)
    lower = jnp.tril(jnp.ones((l, l), dtype=bool))
    decay_in = jnp.exp(jnp.where(lower, acum[..., :, None] - acum[..., None, :], -jnp.inf))
    y_diag = jnp.einsum('bclgn,bcsgn,bgecls,bcsgep->bclgep', cc, bc, decay_in, xc)
    decay_to_end = jnp.exp(acum[..., -1:] - acum)
    chunk_states = jnp.einsum('bclgn,bgecl,bclgep->bcgepn', bc, decay_to_end, xc)
    chunk_decay = jnp.exp(acum[..., -1])

    def step(state, inp):
        st, dec = inp
        return state * dec[..., None, None] + st, state

    h0 = jnp.zeros((b, g, e, p, n), dtype=x.dtype)
    _, prev = lax.scan(step, h0, (chunk_states.transpose(1, 0, 2, 3, 4, 5),
                                  chunk_decay.transpose(3, 0, 1, 2)))
    prev = prev.transpose(1, 0, 2, 3, 4, 5)
    y_off = jnp.einsum('bclgn,bcgepn,bgecl->bclgep', cc, prev, jnp.exp(acum))
    return (y_diag + y_off).reshape(b, s, h, p)


def ssd_mixer(z, xbc, dt, conv_w, conv_b, dt_bias, a_log, d_skip, norm_g, out_dtype):
    b, s, _ = z.shape
    xbc = jax.nn.silu(dwconv_centred(xbc, conv_w, conv_b)).astype(jnp.float32)
    xs, bm, cm = jnp.split(xbc, [SSD_WIDTH, SSD_WIDTH + SSD_GROUPS * SSD_STATE], axis=-1)
    xs = xs.reshape(b, s, SSD_HEADS, SSD_HEAD_DIM)
    bm = bm.reshape(b, s, SSD_GROUPS, SSD_STATE)
    cm = cm.reshape(b, s, SSD_GROUPS, SSD_STATE)
    dt = dt.astype(jnp.float32)
    dtb = dt_bias.astype(jnp.float32)
    dt_f = jax.nn.softplus(dt[..., :SSD_HEADS] + dtb[0])
    dt_b = jax.nn.softplus(dt[..., SSD_HEADS:] + dtb[1])
    a = -jnp.exp(a_log.astype(jnp.float32))
    y_fwd = ssd_scan(xs, dt_f, a[0], bm, cm)
    flip = lambda t: jnp.flip(t, axis=1)
    y_bwd = flip(ssd_scan(flip(xs), flip(dt_b), a[1], flip(bm), flip(cm)))
    y = y_fwd + y_bwd + d_skip.astype(jnp.float32)[:, None] * xs
    y = y.reshape(b, s, SSD_WIDTH) * jax.nn.silu(z.astype(jnp.float32))
    return group_rmsnorm(y, norm_g, SSD_GROUPS, out_dtype)


def short_conv_mixer(gate_b, gate_c, hc, conv_w, conv_b):
    return gate_b * dwconv_centred(gate_c * hc, conv_w, conv_b)


def conv_ffn(h, w_up, conv_w, conv_b, w_down):
    u = dwconv_centred(h @ w_up, conv_w, conv_b)
    gate, up = jnp.split(u, 2, axis=-1)
    return (jax.nn.silu(gate) * up) @ w_down


def setup_inputs(seed: int = 0) -> dict:
    key = jax.random.key(seed)
    ks = jax.random.split(key, 24)
    nrm = lambda k, shape, scale: jax.random.normal(k, shape, jnp.float32) * scale
    gain = lambda k, shape: 1.0 + 0.02 * jax.random.normal(k, shape, jnp.float32)
    dt0 = jnp.exp(jax.random.uniform(ks[5], (DEPTH, 2, SSD_HEADS), jnp.float32,
                                     np.log(1e-3), np.log(1e-1)))
    return {
        'x': jax.random.normal(ks[0], (BATCH, SEQ, D_MODEL), jnp.float32),
        'mix_norm': gain(ks[1], (DEPTH, D_MODEL)),
        'w_in': nrm(ks[2], (DEPTH, D_MODEL, D_IN), D_MODEL ** -0.5),
        'ssd_conv_w': nrm(ks[3], (DEPTH, SSD_CONV, SSD_CONV_DIM), SSD_CONV ** -0.5),
        'ssd_conv_b': nrm(ks[4], (DEPTH, SSD_CONV_DIM), 0.02),
        'ssd_dt_bias': dt0 + jnp.log(-jnp.expm1(-dt0)),
        'ssd_a_log': jnp.log(jax.random.uniform(ks[6], (DEPTH, 2, SSD_HEADS), jnp.float32, 1.0, 16.0)),
        'ssd_d': gain(ks[7], (DEPTH, SSD_HEADS)),
        'ssd_norm': gain(ks[8], (DEPTH, SSD_WIDTH)),
        'sc_conv_w': nrm(ks[9], (DEPTH, SHORT_CONV, CONV_WIDTH), SHORT_CONV ** -0.5),
        'sc_conv_b': nrm(ks[10], (DEPTH, CONV_WIDTH), 0.02),
        'attn_norm': gain(ks[11], (DEPTH, ATT_WIDTH)),
        'sc_norm': gain(ks[12], (DEPTH, CONV_WIDTH)),
        'w_out': nrm(ks[13], (DEPTH, D_MIX, D_MODEL), D_MIX ** -0.5),
        'ffn_norm': gain(ks[14], (DEPTH, D_MODEL)),
        'w_up': nrm(ks[15], (DEPTH, D_MODEL, 2 * D_FF), D_MODEL ** -0.5),
        'ffn_conv_w': nrm(ks[16], (DEPTH, FFN_CONV, 2 * D_FF), FFN_CONV ** -0.5),
        'ffn_conv_b': nrm(ks[17], (DEPTH, 2 * D_FF), 0.02),
        'w_down': nrm(ks[18], (DEPTH, D_FF, D_MODEL), D_FF ** -0.5),
        'final_norm': gain(ks[19], (D_MODEL,)),
    }


def reference(x, mix_norm, w_in, ssd_conv_w, ssd_conv_b, ssd_dt_bias, ssd_a_log, ssd_d,
              ssd_norm, sc_conv_w, sc_conv_b, attn_norm, sc_norm, w_out, ffn_norm, w_up,
              ffn_conv_w, ffn_conv_b, w_down, final_norm):
    for i in range(DEPTH):
        hn = rmsnorm(x, mix_norm[i])
        proj = hn @ w_in[i]
        q, k, v, z, xbc, dt, gate_b, gate_c, hc = jnp.split(proj, SPLIT_POINTS, axis=-1)
        att = group_rmsnorm(dilated_attention(q, k, v), attn_norm[i], ATT_HEADS, x.dtype)
        ssm = ssd_mixer(z, xbc, dt, ssd_conv_w[i], ssd_conv_b[i], ssd_dt_bias[i],
                        ssd_a_log[i], ssd_d[i], ssd_norm[i], x.dtype)
        sc = group_rmsnorm(short_conv_mixer(gate_b, gate_c, hc, sc_conv_w[i], sc_conv_b[i]),
                           sc_norm[i], CONV_GROUPS, x.dtype)
        x = x + jnp.concatenate([att, ssm, sc], axis=-1) @ w_out[i]
        x = x + conv_ffn(rmsnorm(x, ffn_norm[i]), w_up[i], ffn_conv_w[i], ffn_conv_b[i], w_down[i])
    return rmsnorm(x, final_norm)
```

```python
import functools

import numpy as np
import jax
import jax.numpy as jnp
from jax import lax
from jax.experimental import pallas as pl
from jax.experimental.pallas import tpu as pltpu

f32 = jnp.float32
bf16 = jnp.bfloat16

EPS = 1e-6
NEG = -1e30

D_MODEL = 1024
HEAD_DIM = 64
ATT_HEADS = 8
ATT_WIDTH = 512
DILATED_BRANCHES = ((128, 1), (512, 4), (2048, 16))
ATT_HALF = 64
SSD_HEADS = 8
SSD_WIDTH = 512
SSD_GROUPS = 2
SSD_STATE = 128
SSD_CONV = 5
SSD_CHUNK = 128
CONV_WIDTH = 512
CONV_GROUPS = 8
D_MIX = 1536
D_FF = 2816
D_IN = 4624

VMEM_LIMIT = 56 * 1024 * 1024

_NT = (((1,), (1,)), ((), ()))


def _cparams(n_axes):
    return pltpu.CompilerParams(
        dimension_semantics=("arbitrary",) * n_axes, vmem_limit_bytes=VMEM_LIMIT)


def _split2(x):
    hi = x.astype(bf16)
    lo = (x - hi.astype(f32)).astype(bf16)
    return hi, lo


def _split3(x):
    h1 = x.astype(bf16)
    r1 = x - h1.astype(f32)
    h2 = r1.astype(bf16)
    h3 = (r1 - h2.astype(f32)).astype(bf16)
    return h1, h2, h3


def _dot(a, b):
    return jnp.dot(a, b, preferred_element_type=f32)


def _dot_parts(parts, mat):
    acc = _dot(parts[0], mat)
    for p in parts[1:]:
        acc = acc + _dot(p, mat)
    return acc


def _parts_dot(mat, parts):
    acc = _dot(mat, parts[0])
    for p in parts[1:]:
        acc = acc + _dot(mat, p)
    return acc


def _silu(x):
    return x * jax.nn.sigmoid(x)


def _group_ones(width, group):
    idx = np.arange(width) // group
    return jnp.asarray(idx[:, None] == idx[None, :], dtype=bf16)


IN_TM = 512
IN_CHUNK = 512
QKV_W = 3 * ATT_WIDTH
PR_W = 3072
DT_W = 256


def _inproj_body(x_ref, g_ref, w_ref, wdt_ref, qkv_ref, pr_ref, dt_ref):
    x = x_ref[...]
    ms = jnp.mean(x * x, axis=-1, keepdims=True)
    hn = ((x * lax.rsqrt(ms + EPS)) * g_ref[...]).astype(bf16)
    for c0 in range(0, QKV_W, IN_CHUNK):
        qkv_ref[:, c0:c0 + IN_CHUNK] = _dot(hn, w_ref[:, c0:c0 + IN_CHUNK])
    for c0 in range(0, PR_W, IN_CHUNK):
        pr_ref[:, c0:c0 + IN_CHUNK] = _dot(
            hn, w_ref[:, QKV_W + c0:QKV_W + c0 + IN_CHUNK]).astype(bf16)
    dt_ref[...] = _dot(hn, wdt_ref[...])


def _inproj(x2, g, w_main, w_dt):
    m = x2.shape[0]
    const = lambda i: (0, 0)
    return pl.pallas_call(
        _inproj_body,
        grid=(m // IN_TM,),
        in_specs=[
            pl.BlockSpec((IN_TM, D_MODEL), lambda i: (i, 0)),
            pl.BlockSpec((1, D_MODEL), const),
            pl.BlockSpec((D_MODEL, QKV_W + PR_W), const, pipeline_mode=pl.Buffered(1)),
            pl.BlockSpec((D_MODEL, DT_W), const, pipeline_mode=pl.Buffered(1)),
        ],
        out_specs=[
            pl.BlockSpec((IN_TM, QKV_W), lambda i: (i, 0)),
            pl.BlockSpec((IN_TM, PR_W), lambda i: (i, 0)),
            pl.BlockSpec((IN_TM, DT_W), lambda i: (i, 0)),
        ],
        out_shape=[
            jax.ShapeDtypeStruct((m, QKV_W), f32),
            jax.ShapeDtypeStruct((m, PR_W), bf16),
            jax.ShapeDtypeStruct((m, DT_W), f32),
        ],
        compiler_params=_cparams(1),
        name="inproj",
    )(x2, g, w_main, w_dt)


ATT_TQ = 128
ATT_TK = ATT_TQ + 2 * ATT_HALF
ATT_FIN = 256


def _attn_body(slopes_ref, q_ref, k_ref, v_ref, g_ref, gmat_ref, o_ref, ob_ref, lse_ref, *, seq):
    pair = pl.program_id(1)
    lane = lax.broadcasted_iota(jnp.int32, (1, 2 * HEAD_DIM), 1)
    head0 = lane < HEAD_DIM
    qmask = (jnp.where(head0, HEAD_DIM ** -0.5, 0.0), jnp.where(head0, 0.0, HEAD_DIM ** -0.5))
    slope = (slopes_ref[2 * pair], slopes_ref[2 * pair + 1])
    base = (lax.broadcasted_iota(jnp.int32, (ATT_TQ, ATT_TK), 1)
            - lax.broadcasted_iota(jnp.int32, (ATT_TQ, ATT_TK), 0))

    for bi, (_, r) in enumerate(DILATED_BRANCHES):
        sub_len = seq // r
        tiles_per_class = sub_len // ATT_TQ

        def tile(t, carry, bi=bi, r=r, sub_len=sub_len, tiles_per_class=tiles_per_class):
            cls = t // tiles_per_class
            t0 = (t % tiles_per_class) * ATT_TQ
            k0 = jnp.clip(t0 - ATT_HALF, 0, sub_len - ATT_TK)
            q_rows = pl.ds(cls + r * t0, ATT_TQ, stride=r)
            k_rows = pl.ds(cls + r * k0, ATT_TK, stride=r)
            q = q_ref[q_rows, :]
            k = k_ref[k_rows, :].astype(bf16)
            v = v_ref[k_rows, :].astype(bf16)
            dist = jnp.abs(base + (k0 - t0))
            valid = dist <= ATT_HALF
            nbias = dist.astype(f32) * (-float(r))
            o_pair, lse_pair = None, None
            for hh in range(2):
                s = lax.dot_general((q * qmask[hh]).astype(bf16), k, _NT,
                                    preferred_element_type=f32)
                s = jnp.where(valid, s + slope[hh] * nbias, NEG)
                m = jnp.max(s, axis=-1, keepdims=True)
                p = jnp.exp(s - m)
                l = jnp.sum(p, axis=-1, keepdims=True)
                o = _dot(p.astype(bf16), v) / l
                lse = jnp.broadcast_to(m + jnp.log(l), o.shape)
                if hh == 0:
                    o_pair, lse_pair = o, lse
                else:
                    o_pair = jnp.where(head0, o_pair, o)
                    lse_pair = jnp.where(head0, lse_pair, lse)
            ob_ref[bi, q_rows, :] = o_pair
            lse_ref[bi, q_rows, :] = lse_pair
            return carry

        lax.fori_loop(0, seq // ATT_TQ, tile, 0)

    gmat = gmat_ref[...]

    def combine(t, carry):
        rows = pl.ds(pl.multiple_of(t * ATT_FIN, ATT_FIN), ATT_FIN)
        l0, l1, l2 = lse_ref[0, rows, :], lse_ref[1, rows, :], lse_ref[2, rows, :]
        mx = jnp.maximum(jnp.maximum(l0, l1), l2)
        e0, e1, e2 = jnp.exp(l0 - mx), jnp.exp(l1 - mx), jnp.exp(l2 - mx)
        y = (e0 * ob_ref[0, rows, :] + e1 * ob_ref[1, rows, :] + e2 * ob_ref[2, rows, :]) / (e0 + e1 + e2)
        ms = _dot_parts(_split2(y * y), gmat) * (1.0 / HEAD_DIM)
        o_ref[rows, :] = (y * lax.rsqrt(ms + EPS) * g_ref[...]).astype(o_ref.dtype)
        return carry

    lax.fori_loop(0, seq // ATT_FIN, combine, 0)


def _attention(qkv, attn_norm_row, slopes, batch, seq):
    m = qkv.shape[0]
    pairs = ATT_HEADS // 2
    w = 2 * HEAD_DIM
    return pl.pallas_call(
        functools.partial(_attn_body, seq=seq),
        grid=(batch, pairs),
        in_specs=[
            pl.BlockSpec(memory_space=pltpu.SMEM),
            pl.BlockSpec((seq, w), lambda b, p: (b, p)),
            pl.BlockSpec((seq, w), lambda b, p: (b, pairs + p)),
            pl.BlockSpec((seq, w), lambda b, p: (b, 2 * pairs + p)),
            pl.BlockSpec((1, w), lambda b, p: (0, p)),
            pl.BlockSpec((w, w), lambda b, p: (0, 0)),
        ],
        out_specs=pl.BlockSpec((seq, w), lambda b, p: (b, p)),
        out_shape=jax.ShapeDtypeStruct((m, ATT_WIDTH), bf16),
        scratch_shapes=[
            pltpu.VMEM((len(DILATED_BRANCHES), seq, w), f32),
            pltpu.VMEM((len(DILATED_BRANCHES), seq, w), f32),
        ],
        compiler_params=_cparams(2),
        name="dilated_attention",
    )(slopes, qkv, qkv, qkv, attn_norm_row, _group_ones(w, HEAD_DIM))


SSD_L = SSD_CHUNK
SSD_GW = SSD_WIDTH // SSD_GROUPS
SSD_E = SSD_HEADS // SSD_GROUPS
SSD_XC = SSD_GW + 2 * SSD_STATE
SSD_HALO = 16


def _ssd_body(z_ref, xs_ref, b_ref, c_ref, dt_ref, cw_ref, cb_ref, dtb_ref, alog_ref,
              dsk_ref, ng_ref, expand_ref, o_ref, stage_ref, xc_ref, y_ref, st_ref, *, seq):
    n_chunks = seq // SSD_L
    L = SSD_L
    row_i = lax.broadcasted_iota(jnp.int32, (L, L), 0)
    col_i = lax.broadcasted_iota(jnp.int32, (L, L), 1)
    lower = row_i >= col_i
    upper = row_i <= col_i
    lower_b = lower.astype(bf16)
    upper_b = upper.astype(bf16)
    lane8 = lax.broadcasted_iota(jnp.int32, (1, L), 1)
    fwd_lane = lane8 < SSD_E
    sub8 = lax.broadcasted_iota(jnp.int32, (L, 1), 0)
    fwd_row = sub8 < SSD_E
    a_row = -jnp.exp(alog_ref[...])
    expand = expand_ref[...]
    lane_blk = lax.broadcasted_iota(jnp.int32, (1, SSD_GW), 1) // 64

    def conv_chunk(c):
        r0 = pl.multiple_of(c * L, L)
        rp = pl.multiple_of(jnp.maximum(r0 - SSD_HALO, 0), SSD_HALO)
        rn = pl.multiple_of(jnp.minimum(r0 + L, seq - SSD_HALO), SSD_HALO)
        keep_p = jnp.where(c > 0, 1.0, 0.0)
        keep_n = jnp.where(c < n_chunks - 1, 1.0, 0.0)
        for ref, lo, hi in ((xs_ref, 0, SSD_GW), (b_ref, SSD_GW, SSD_GW + SSD_STATE),
                            (c_ref, SSD_GW + SSD_STATE, SSD_XC)):
            stage_ref[0:SSD_HALO, lo:hi] = ref[pl.ds(rp, SSD_HALO), :].astype(f32) * keep_p
            stage_ref[SSD_HALO:SSD_HALO + L, lo:hi] = ref[pl.ds(r0, L), :].astype(f32)
            stage_ref[SSD_HALO + L:, lo:hi] = ref[pl.ds(rn, SSD_HALO), :].astype(f32) * keep_n
        acc = cb_ref[...] + cw_ref[0:1, :] * stage_ref[SSD_HALO - 2:SSD_HALO - 2 + L, :]
        for kk in range(1, SSD_CONV):
            o0 = SSD_HALO - 2 + kk
            acc = acc + cw_ref[kk:kk + 1, :] * stage_ref[o0:o0 + L, :]
        return _silu(acc)

    def dt_terms(c):
        r0 = pl.multiple_of(c * L, L)
        dtv = jax.nn.softplus(dt_ref[pl.ds(r0, L), :] + dtb_ref[...])
        adt = dtv * a_row
        parts = _split3(adt)
        cum = jnp.where(fwd_lane, _parts_dot(lower_b, parts), _parts_dot(upper_b, parts))
        tot = jnp.sum(adt, axis=0, keepdims=True)
        return dtv, adt, cum, tot

    def expand_lanes(a):
        return _dot_parts(_split2(a), expand)

    st_ref[...] = jnp.zeros_like(st_ref)

    def pass_a(c, carry):
        r0 = pl.multiple_of(c * L, L)
        xc = conv_chunk(c)
        xc_ref[pl.ds(r0, L), :] = xc.astype(bf16)
        xs = xc[:, 0:SSD_GW]
        bm = xc[:, SSD_GW:SSD_GW + SSD_STATE].astype(bf16)
        cm = xc[:, SSD_GW + SSD_STATE:SSD_XC].astype(bf16)
        dtv, adt, cum, tot = dt_terms(c)
        adt_t = adt.T
        parts_t = _split3(adt_t)
        cum_t = jnp.where(fwd_row, _dot_parts(parts_t, upper_b), _dot_parts(parts_t, lower_b))
        gram = lax.dot_general(cm, bm, _NT, preferred_element_type=f32)
        dt_x = expand_lanes(dtv)
        x_cat = jnp.concatenate(
            [(xs * dt_x[:, 0:SSD_GW]).astype(bf16), (xs * dt_x[:, SSD_GW:]).astype(bf16)], axis=0)
        m_rows = []
        for e in range(SSD_E):
            df = jnp.exp(jnp.where(lower, cum[:, e:e + 1] - cum_t[e:e + 1, :], NEG))
            db = jnp.exp(jnp.where(upper, cum[:, SSD_E + e:SSD_E + e + 1]
                                   - cum_t[SSD_E + e:SSD_E + e + 1, :], NEG))
            m_rows.append(jnp.concatenate([(gram * df).astype(bf16), (gram * db).astype(bf16)], axis=1))
        y_all = _dot(jnp.concatenate(m_rows, axis=0), x_cat)
        y = jnp.zeros((L, SSD_GW), f32)
        for e in range(SSD_E):
            y = jnp.where(lane_blk == e, y_all[e * L:(e + 1) * L, :], y)
        ecum_x = expand_lanes(jnp.exp(cum))
        state = st_ref[...]
        y = y + _dot(cm, state.astype(bf16)) * ecum_x[:, 0:SSD_GW]
        y_ref[pl.ds(r0, L), :] = y
        w_x = expand_lanes(jnp.exp(tot - cum) * dtv)
        xin = (xs * w_x[:, 0:SSD_GW]).astype(bf16)
        dec = expand_lanes(jnp.broadcast_to(jnp.exp(tot), (8, L)))[0:1, 0:SSD_GW]
        st_ref[...] = state * dec + lax.dot_general(
            bm, xin, (((0,), (0,)), ((), ())), preferred_element_type=f32)
        return carry

    lax.fori_loop(0, n_chunks, pass_a, 0)

    st_ref[...] = jnp.zeros_like(st_ref)

    def pass_b(i, carry):
        c = n_chunks - 1 - i
        r0 = pl.multiple_of(c * L, L)
        xcb = xc_ref[pl.ds(r0, L), :]
        xs = xcb[:, 0:SSD_GW].astype(f32)
        bm = xcb[:, SSD_GW:SSD_GW + SSD_STATE]
        cm = xcb[:, SSD_GW + SSD_STATE:SSD_XC]
        dtv, adt, cum, tot = dt_terms(c)
        ecum_x = expand_lanes(jnp.exp(cum))
        state = st_ref[...]
        y = y_ref[pl.ds(r0, L), :] + _dot(cm, state.astype(bf16)) * ecum_x[:, SSD_GW:]
        w_x = expand_lanes(jnp.exp(tot - cum) * dtv)
        xin = (xs * w_x[:, SSD_GW:]).astype(bf16)
        dec = expand_lanes(jnp.broadcast_to(jnp.exp(tot), (8, L)))[0:1, SSD_GW:]
        st_ref[...] = state * dec + lax.dot_general(
            bm, xin, (((0,), (0,)), ((), ())), preferred_element_type=f32)
        y = y + dsk_ref[...] * xs
        y = y * _silu(z_ref[pl.ds(r0, L), :].astype(f32))
        ms = jnp.mean(y * y, axis=-1, keepdims=True)
        o_ref[pl.ds(r0, L), :] = (y * lax.rsqrt(ms + EPS) * ng_ref[...]).astype(o_ref.dtype)
        return carry

    lax.fori_loop(0, n_chunks, pass_b, 0)


def _ssd(pr, dt, conv_w_g, conv_b_g, dtb_row, alog_row, dskip_row, norm_row, batch, seq):
    m = pr.shape[0]
    expand = np.zeros((SSD_L, 2 * SSD_GW), np.float32)
    for e in range(SSD_E):
        expand[e, 64 * e:64 * (e + 1)] = 1.0
        expand[SSD_E + e, SSD_GW + 64 * e:SSD_GW + 64 * (e + 1)] = 1.0
    expand = jnp.asarray(expand, dtype=bf16)
    return pl.pallas_call(
        functools.partial(_ssd_body, seq=seq),
        grid=(batch, SSD_GROUPS),
        in_specs=[
            pl.BlockSpec((seq, SSD_GW), lambda b, g: (b, g)),
            pl.BlockSpec((seq, SSD_GW), lambda b, g: (b, 2 + g)),
            pl.BlockSpec((seq, SSD_STATE), lambda b, g: (b, 8 + g)),
            pl.BlockSpec((seq, SSD_STATE), lambda b, g: (b, 10 + g)),
            pl.BlockSpec((seq, 128), lambda b, g: (b, g)),
            pl.BlockSpec((None, SSD_CONV, SSD_XC), lambda b, g: (g, 0, 0)),
            pl.BlockSpec((None, 1, SSD_XC), lambda b, g: (g, 0, 0)),
            pl.BlockSpec((None, 1, 128), lambda b, g: (g, 0, 0)),
            pl.BlockSpec((None, 1, 128), lambda b, g: (g, 0, 0)),
            pl.BlockSpec((None, 1, SSD_GW), lambda b, g: (g, 0, 0)),
            pl.BlockSpec((None, 1, SSD_GW), lambda b, g: (g, 0, 0)),
            pl.BlockSpec((SSD_L, 2 * SSD_GW), lambda b, g: (0, 0)),
        ],
        out_specs=pl.BlockSpec((seq, SSD_GW), lambda b, g: (b, g)),
        out_shape=jax.ShapeDtypeStruct((m, SSD_WIDTH), bf16),
        scratch_shapes=[
            pltpu.VMEM((SSD_L + 2 * SSD_HALO, SSD_XC), f32),
            pltpu.VMEM((seq, SSD_XC), bf16),
            pltpu.VMEM((seq, SSD_GW), f32),
            pltpu.VMEM((SSD_STATE, SSD_GW), f32),
        ],
        compiler_params=_cparams(2),
        name="ssd",
    )(pr, pr, pr, pr, dt, conv_w_g, conv_b_g, dtb_row, alog_row, dskip_row, norm_row, expand)


OUT_TM = 512
OUT_HALO = 16


def _outproj_body(att_ref, ssm_ref, gb_ref, gcp_ref, gc_ref, gcn_ref, hcp_ref, hc_ref, hcn_ref,
                  x_ref, w_ref, cw_ref, cb_ref, ng_ref, gmat_ref, o_ref, stage_ref, *, tiles_per_seq):
    i = pl.program_id(0)
    keep_p = jnp.where(i % tiles_per_seq > 0, 1.0, 0.0)
    keep_n = jnp.where(i % tiles_per_seq < tiles_per_seq - 1, 1.0, 0.0)
    tm = OUT_TM
    h = OUT_HALO
    stage_ref[0:h, :] = gcp_ref[...].astype(f32) * hcp_ref[...].astype(f32) * keep_p
    stage_ref[h:h + tm, :] = gc_ref[...].astype(f32) * hc_ref[...].astype(f32)
    stage_ref[h + tm:, :] = gcn_ref[...].astype(f32) * hcn_ref[...].astype(f32) * keep_n
    conv = (cb_ref[...] + cw_ref[0:1, :] * stage_ref[h - 1:h - 1 + tm, :]
            + cw_ref[1:2, :] * stage_ref[h:h + tm, :]
            + cw_ref[2:3, :] * stage_ref[h + 1:h + 1 + tm, :])
    y = gb_ref[...].astype(f32) * conv
    ms = _dot_parts(_split2(y * y), gmat_ref[...]) * (1.0 / (CONV_WIDTH // CONV_GROUPS))
    sc = (y * lax.rsqrt(ms + EPS) * ng_ref[...]).astype(bf16)
    mix = jnp.concatenate([att_ref[...], ssm_ref[...], sc], axis=-1)
    o_ref[...] = x_ref[...] + _dot(mix, w_ref[...])


def _outproj(att, ssm, pr, x2, w_out, sc_w, sc_b, sc_g, seq):
    m = x2.shape[0]
    tm, h = OUT_TM, OUT_HALO
    per = tm // h
    nb = m // h
    const = lambda i: (0, 0)
    prev = lambda col: (lambda i: (jnp.maximum(i * per - 1, 0), col))
    nxt = lambda col: (lambda i: (jnp.minimum((i + 1) * per, nb - 1), col))
    return pl.pallas_call(
        functools.partial(_outproj_body, tiles_per_seq=seq // tm),
        grid=(m // tm,),
        in_specs=[
            pl.BlockSpec((tm, ATT_WIDTH), lambda i: (i, 0)),
            pl.BlockSpec((tm, SSD_WIDTH), lambda i: (i, 0)),
            pl.BlockSpec((tm, CONV_WIDTH), lambda i: (i, 3)),
            pl.BlockSpec((h, CONV_WIDTH), prev(4)),
            pl.BlockSpec((tm, CONV_WIDTH), lambda i: (i, 4)),
            pl.BlockSpec((h, CONV_WIDTH), nxt(4)),
            pl.BlockSpec((h, CONV_WIDTH), prev(5)),
            pl.BlockSpec((tm, CONV_WIDTH), lambda i: (i, 5)),
            pl.BlockSpec((h, CONV_WIDTH), nxt(5)),
            pl.BlockSpec((tm, D_MODEL), lambda i: (i, 0)),
            pl.BlockSpec((D_MIX, D_MODEL), const, pipeline_mode=pl.Buffered(1)),
            pl.BlockSpec((3, CONV_WIDTH), const),
            pl.BlockSpec((1, CONV_WIDTH), const),
            pl.BlockSpec((1, CONV_WIDTH), const),
            pl.BlockSpec((CONV_WIDTH, CONV_WIDTH), const),
        ],
        out_specs=pl.BlockSpec((tm, D_MODEL), lambda i: (i, 0)),
        out_shape=jax.ShapeDtypeStruct((m, D_MODEL), f32),
        scratch_shapes=[pltpu.VMEM((tm + 2 * h, CONV_WIDTH), f32)],
        compiler_params=_cparams(1),
        name="outproj",
    )(att, ssm, pr, pr, pr, pr, pr, pr, pr, x2, w_out, sc_w, sc_b, sc_g,
      _group_ones(CONV_WIDTH, CONV_WIDTH // CONV_GROUPS))


FFN_TM = 512
FFN_HALO = 8
FFN_CHUNK = 256


def _ffn_body(xp_ref, x_ref, xn_ref, g_ref, wup_ref, cw_ref, cb_ref, wdn_ref, fg_ref, o_ref,
              hn_ref, u_ref, act_ref, *, tiles_per_seq, final_norm):
    i = pl.program_id(0)
    tm, h = FFN_TM, FFN_HALO
    keep_p = jnp.where(i % tiles_per_seq > 0, 1.0, 0.0)
    keep_n = jnp.where(i % tiles_per_seq < tiles_per_seq - 1, 1.0, 0.0)
    xall = jnp.concatenate([xp_ref[...], x_ref[...], xn_ref[...]], axis=0)
    ms = jnp.mean(xall * xall, axis=-1, keepdims=True)
    hn_ref[...] = ((xall * lax.rsqrt(ms + EPS)) * g_ref[...]).astype(bf16)
    row = lax.broadcasted_iota(jnp.int32, (tm + 2 * h, 1), 0)
    keep = jnp.where(row < h, keep_p, jnp.where(row >= tm + h, keep_n, 1.0))
    for c0 in range(0, D_FF, FFN_CHUNK):
        halves = []
        for part in range(2):
            col = part * D_FF + c0
            u_ref[...] = _dot(hn_ref[...], wup_ref[:, col:col + FFN_CHUNK]) * keep
            halves.append(cb_ref[:, col:col + FFN_CHUNK]
                          + cw_ref[0:1, col:col + FFN_CHUNK] * u_ref[h - 1:h - 1 + tm, :]
                          + cw_ref[1:2, col:col + FFN_CHUNK] * u_ref[h:h + tm, :]
                          + cw_ref[2:3, col:col + FFN_CHUNK] * u_ref[h + 1:h + 1 + tm, :])
        act_ref[:, c0:c0 + FFN_CHUNK] = (_silu(halves[0]) * halves[1]).astype(bf16)
    out = x_ref[...] + _dot(act_ref[...], wdn_ref[...])
    if final_norm:
        ms2 = jnp.mean(out * out, axis=-1, keepdims=True)
        out = (out * lax.rsqrt(ms2 + EPS)) * fg_ref[...]
    o_ref[...] = out


def _ffn(x2, g, w_up, conv_w, conv_b, w_down, final_g, seq, final_norm):
    m = x2.shape[0]
    tm, h = FFN_TM, FFN_HALO
    per = tm // h
    nb = m // h
    const = lambda i: (0, 0)
    return pl.pallas_call(
        functools.partial(_ffn_body, tiles_per_seq=seq // tm, final_norm=final_norm),
        grid=(m // tm,),
        in_specs=[
            pl.BlockSpec((h, D_MODEL), lambda i: (jnp.maximum(i * per - 1, 0), 0)),
            pl.BlockSpec((tm, D_MODEL), lambda i: (i, 0)),
            pl.BlockSpec((h, D_MODEL), lambda i: (jnp.minimum((i + 1) * per, nb - 1), 0)),
            pl.BlockSpec((1, D_MODEL), const),
            pl.BlockSpec((D_MODEL, 2 * D_FF), const, pipeline_mode=pl.Buffered(1)),
            pl.BlockSpec((3, 2 * D_FF), const),
            pl.BlockSpec((1, 2 * D_FF), const),
            pl.BlockSpec((D_FF, D_MODEL), const, pipeline_mode=pl.Buffered(1)),
            pl.BlockSpec((1, D_MODEL), const),
        ],
        out_specs=pl.BlockSpec((tm, D_MODEL), lambda i: (i, 0)),
        out_shape=jax.ShapeDtypeStruct((m, D_MODEL), f32),
        scratch_shapes=[
            pltpu.VMEM((tm + 2 * h, D_MODEL), bf16),
            pltpu.VMEM((tm + 2 * h, FFN_CHUNK), f32),
            pltpu.VMEM((tm, D_FF), bf16),
        ],
        compiler_params=_cparams(1),
        name="convffn",
    )(x2, x2, x2, g, w_up, conv_w, conv_b, w_down, final_g)


def _layer_params(i, w_in, ssd_conv_w, ssd_conv_b, ssd_dt_bias, ssd_a_log, ssd_d, ssd_norm):
    w = w_in[i]
    dt0 = 3 * ATT_WIDTH + SSD_WIDTH + SSD_WIDTH + 2 * SSD_GROUPS * SSD_STATE
    w_main = jnp.concatenate([w[:, :dt0], w[:, dt0 + 2 * SSD_HEADS:]], axis=1).astype(bf16)
    w_dt_cols = w[:, dt0:dt0 + 2 * SSD_HEADS]

    def per_group_lanes(v2):
        v = v2.reshape(2, SSD_GROUPS, SSD_E).transpose(1, 0, 2).reshape(SSD_GROUPS, 1, 2 * SSD_E)
        return jnp.pad(v, ((0, 0), (0, 0), (0, 128 - 2 * SSD_E)))

    wd = w_dt_cols.reshape(D_MODEL, 2, SSD_GROUPS, SSD_E).transpose(0, 2, 1, 3)
    wd = wd.reshape(D_MODEL, SSD_GROUPS, 2 * SSD_E)
    w_dt = jnp.pad(wd, ((0, 0), (0, 0), (0, 128 - 2 * SSD_E))).reshape(D_MODEL, DT_W).astype(bf16)

    cw, cb = ssd_conv_w[i], ssd_conv_b[i]
    def group_channels(a):
        outs = []
        for g in range(SSD_GROUPS):
            outs.append(jnp.concatenate([
                a[..., g * SSD_GW:(g + 1) * SSD_GW],
                a[..., SSD_WIDTH + g * SSD_STATE:SSD_WIDTH + (g + 1) * SSD_STATE],
                a[..., SSD_WIDTH + (SSD_GROUPS + g) * SSD_STATE:SSD_WIDTH + (SSD_GROUPS + g + 1) * SSD_STATE],
            ], axis=-1))
        return jnp.stack(outs, axis=0)

    conv_w_g = group_channels(cw)
    conv_b_g = group_channels(cb[None, :])
    dtb_row = per_group_lanes(ssd_dt_bias[i])
    alog_row = per_group_lanes(ssd_a_log[i])
    dskip_row = jnp.repeat(ssd_d[i], HEAD_DIM).reshape(SSD_GROUPS, 1, SSD_GW)
    norm_row = ssd_norm[i].reshape(SSD_GROUPS, 1, SSD_GW)
    return w_main, w_dt, conv_w_g, conv_b_g, dtb_row, alog_row, dskip_row, norm_row


def kernel(x, mix_norm, w_in, ssd_conv_w, ssd_conv_b, ssd_dt_bias, ssd_a_log, ssd_d, ssd_norm, sc_conv_w, sc_conv_b, attn_norm, sc_norm, w_out, ffn_norm, w_up, ffn_conv_w, ffn_conv_b, w_down, final_norm):
    batch, seq, d = x.shape
    depth = w_in.shape[0]
    x2 = x.reshape(batch * seq, d)
    slopes = jnp.asarray(2.0 ** (-8.0 * (np.arange(ATT_HEADS) + 1.0) / ATT_HEADS), dtype=f32)
    for i in range(depth):
        (w_main, w_dt, conv_w_g, conv_b_g, dtb_row, alog_row, dskip_row, norm_row) = _layer_params(
            i, w_in, ssd_conv_w, ssd_conv_b, ssd_dt_bias, ssd_a_log, ssd_d, ssd_norm)
        qkv, pr, dt = _inproj(x2, mix_norm[i][None, :], w_main, w_dt)
        att = _attention(qkv, attn_norm[i][None, :], slopes, batch, seq)
        ssm = _ssd(pr, dt, conv_w_g, conv_b_g, dtb_row, alog_row, dskip_row, norm_row, batch, seq)
        x2 = _outproj(att, ssm, pr, x2, w_out[i].astype(bf16), sc_conv_w[i], sc_conv_b[i][None, :],
                      sc_norm[i][None, :], seq)
        x2 = _ffn(x2, ffn_norm[i][None, :], w_up[i].astype(bf16), ffn_conv_w[i],
                  ffn_conv_b[i][None, :], w_down[i].astype(bf16), final_norm[None, :], seq,
                  final_norm=(i == depth - 1))
    return x2.reshape(batch, seq, d)
```

```python
import functools

import numpy as np
import jax
import jax.numpy as jnp
from jax import lax
from jax.experimental import pallas as pl
from jax.experimental.pallas import tpu as pltpu

f32 = jnp.float32
bf16 = jnp.bfloat16

EPS = 1e-6
NEG = -1e30

D_MODEL = 1024
HEAD_DIM = 64
ATT_HEADS = 8
ATT_WIDTH = 512
DILATED_BRANCHES = ((128, 1), (512, 4), (2048, 16))
ATT_HALF = 64
SSD_HEADS = 8
SSD_WIDTH = 512
SSD_GROUPS = 2
SSD_STATE = 128
SSD_CONV = 5
SSD_CHUNK = 128
CONV_WIDTH = 512
CONV_GROUPS = 8
D_MIX = 1536
D_FF = 2816
D_IN = 4624

VMEM_LIMIT = 56 * 1024 * 1024
LANES = 128

_NT = (((1,), (1,)), ((), ()))


def _cparams(n_axes):
    return pltpu.CompilerParams(
        dimension_semantics=("arbitrary",) * n_axes, vmem_limit_bytes=VMEM_LIMIT)


def _split2(x):
    hi = x.astype(bf16)
    lo = (x - hi.astype(f32)).astype(bf16)
    return hi, lo


def _split3(x):
    h1 = x.astype(bf16)
    r1 = x - h1.astype(f32)
    h2 = r1.astype(bf16)
    h3 = (r1 - h2.astype(f32)).astype(bf16)
    return h1, h2, h3


def _dot(a, b):
    return jnp.dot(a, b, preferred_element_type=f32)


def _dot_parts(parts, mat):
    acc = _dot(parts[0], mat)
    for p in parts[1:]:
        acc = acc + _dot(p, mat)
    return acc


def _parts_dot(mat, parts):
    acc = _dot(mat, parts[0])
    for p in parts[1:]:
        acc = acc + _dot(mat, p)
    return acc


def _silu(x):
    return x * jax.nn.sigmoid(x)


def _group_ones(width, group):
    idx = np.arange(width) // group
    return jnp.asarray(idx[:, None] == idx[None, :], dtype=bf16)


IN_TM = 512
IN_CHUNK = 512
QKV_W = 3 * ATT_WIDTH
PR_W = 3072
DT_W = 256


def _inproj_body(x_ref, g_ref, w_ref, wdt_ref, qkv_ref, pr_ref, dt_ref):
    x = x_ref[...]
    ms = jnp.mean(x * x, axis=-1, keepdims=True)
    hn = ((x * lax.rsqrt(ms + EPS)) * g_ref[...]).astype(bf16)
    for c0 in range(0, QKV_W, IN_CHUNK):
        qkv_ref[:, c0:c0 + IN_CHUNK] = _dot(hn, w_ref[:, c0:c0 + IN_CHUNK])
    for c0 in range(0, PR_W, IN_CHUNK):
        pr_ref[:, c0:c0 + IN_CHUNK] = _dot(
            hn, w_ref[:, QKV_W + c0:QKV_W + c0 + IN_CHUNK]).astype(bf16)
    dt_ref[...] = _dot(hn, wdt_ref[...])


def _inproj(x2, g, w_main, w_dt):
    m = x2.shape[0]
    const = lambda i: (0, 0)
    return pl.pallas_call(
        _inproj_body,
        grid=(m // IN_TM,),
        in_specs=[
            pl.BlockSpec((IN_TM, D_MODEL), lambda i: (i, 0)),
            pl.BlockSpec((1, D_MODEL), const),
            pl.BlockSpec((D_MODEL, QKV_W + PR_W), const, pipeline_mode=pl.Buffered(1)),
            pl.BlockSpec((D_MODEL, DT_W), const, pipeline_mode=pl.Buffered(1)),
        ],
        out_specs=[
            pl.BlockSpec((IN_TM, QKV_W), lambda i: (i, 0)),
            pl.BlockSpec((IN_TM, PR_W), lambda i: (i, 0)),
            pl.BlockSpec((IN_TM, DT_W), lambda i: (i, 0)),
        ],
        out_shape=[
            jax.ShapeDtypeStruct((m, QKV_W), f32),
            jax.ShapeDtypeStruct((m, PR_W), bf16),
            jax.ShapeDtypeStruct((m, DT_W), f32),
        ],
        compiler_params=_cparams(1),
        name="inproj",
    )(x2, g, w_main, w_dt)


ATT_TQ = 128
ATT_TK = ATT_TQ + 2 * ATT_HALF
ATT_FIN = 256
ATT_UNROLL = 8
ATT_OFFSETS = (0, -ATT_HALF, -2 * ATT_HALF)
ATT_PRE = 4


def _attn_body(slopes_ref, q_ref, k_ref, v_ref, g_ref, gmat_ref, o_ref,
               acc_ref, m_ref, l_ref, bias_ref, d4_ref, *, seq):
    pair = pl.program_id(1)
    lane = lax.broadcasted_iota(jnp.int32, (1, 2 * HEAD_DIM), 1)
    head0 = lane < HEAD_DIM
    qmask0 = jnp.where(head0, HEAD_DIM ** -0.5, 0.0)
    qmask1 = jnp.where(head0, 0.0, HEAD_DIM ** -0.5)

    row = lax.broadcasted_iota(jnp.int32, (2 * ATT_TQ, ATT_TK), 0)
    col = lax.broadcasted_iota(jnp.int32, (2 * ATT_TQ, ATT_TK), 1)
    rel0 = col - (row & (ATT_TQ - 1))
    slope_rows = jnp.where(row < ATT_TQ, slopes_ref[2 * pair], slopes_ref[2 * pair + 1])
    for bi, (_, r) in enumerate(DILATED_BRANCHES):
        for vi, off in enumerate(ATT_OFFSETS):
            dist = jnp.abs(rel0 + off)
            bias_ref[bi, vi] = jnp.where(dist <= ATT_HALF, dist.astype(f32) * (slope_rows * (-float(r))), NEG)

    pre_len = seq // ATT_PRE

    def deinterleave(j, carry):
        for a, ref in enumerate((q_ref, k_ref, v_ref)):
            for c in range(ATT_PRE):
                d4_ref[a, pl.ds(c * pre_len + pl.multiple_of(j * ATT_FIN, ATT_FIN), ATT_FIN), :] = (
                    ref[pl.ds(c + ATT_PRE * pl.multiple_of(j * ATT_FIN, ATT_FIN), ATT_FIN, stride=ATT_PRE), :])
        return carry

    lax.fori_loop(0, pre_len // ATT_FIN, deinterleave, 0)

    for bi, (_, r) in enumerate(DILATED_BRANCHES):
        sub_len = seq // r
        tiles_per_class = sub_len // ATT_TQ

        def scores(t, r=r, sub_len=sub_len, tiles_per_class=tiles_per_class):
            cls = t // tiles_per_class
            t0 = (t % tiles_per_class) * ATT_TQ
            k0 = jnp.clip(t0 - ATT_HALF, 0, sub_len - ATT_TK)
            variant = jnp.where(t0 == 0, 0, jnp.where(t0 == sub_len - ATT_TQ, 2, 1))
            q_rows = pl.ds(cls + r * t0, ATT_TQ, stride=r)
            if r > ATT_PRE:
                base = (cls % ATT_PRE) * (seq // ATT_PRE) + cls // ATT_PRE
                q = d4_ref[0, pl.ds(base + (r // ATT_PRE) * t0, ATT_TQ, stride=r // ATT_PRE), :]
                k_rows = pl.ds(base + (r // ATT_PRE) * k0, ATT_TK, stride=r // ATT_PRE)
                k = d4_ref[1, k_rows, :]
            else:
                q = q_ref[q_rows, :]
                k_rows = pl.ds(cls + r * k0, ATT_TK, stride=r)
                k = k_ref[k_rows, :]
            q2 = jnp.concatenate([q * qmask0, q * qmask1], axis=0).astype(bf16)
            s = lax.dot_general(q2, k.astype(bf16), _NT, preferred_element_type=f32)
            return q_rows, k_rows, variant, s

        def softmax(unit, bi=bi):
            q_rows, k_rows, variant, s = unit
            s = s + bias_ref[bi, variant]
            m = jnp.max(s, axis=-1, keepdims=True)
            return q_rows, k_rows, jnp.exp(s - m).astype(bf16), m

        def values(unit, bi=bi, r=r):
            q_rows, k_rows, p, m = unit
            v = d4_ref[2, k_rows, :] if r > ATT_PRE else v_ref[k_rows, :]
            o0 = _dot(p[0:ATT_TQ], jnp.where(head0, v, 1.0).astype(bf16))
            o1 = _dot(p[ATT_TQ:], jnp.where(head0, 1.0, v).astype(bf16))
            acc_ref[bi, q_rows, :] = jnp.where(head0, o0, o1)
            l_ref[bi, q_rows, :] = jnp.where(head0, o1, o0)
            m_ref[bi, q_rows, :] = jnp.where(head0, m[0:ATT_TQ], m[ATT_TQ:])

        def tile_group(g, carry):
            st_s, st_p = {}, {}
            for step in range(ATT_UNROLL + 2):
                if step < ATT_UNROLL:
                    st_s[step] = scores(g * ATT_UNROLL + step)
                if 0 <= step - 1 < ATT_UNROLL:
                    st_p[step - 1] = softmax(st_s.pop(step - 1))
                if 0 <= step - 2 < ATT_UNROLL:
                    values(st_p.pop(step - 2))
            return carry

        lax.fori_loop(0, seq // (ATT_TQ * ATT_UNROLL), tile_group, 0)

    gmat = gmat_ref[...]

    def combine(t, carry):
        rows = pl.ds(pl.multiple_of(t * ATT_FIN, ATT_FIN), ATT_FIN)
        ms_ = [m_ref[b, rows, :] for b in range(len(DILATED_BRANCHES))]
        mx = functools.reduce(jnp.maximum, ms_)
        num, den = None, None
        for b, mb in enumerate(ms_):
            e = jnp.exp(mb - mx)
            nb = e * acc_ref[b, rows, :]
            db = e * pltpu.roll(l_ref[b, rows, :], HEAD_DIM, axis=1)
            num = nb if num is None else num + nb
            den = db if den is None else den + db
        y = num / den
        ms = _dot_parts(_split2(y * y), gmat) * (1.0 / HEAD_DIM)
        o_ref[rows, :] = (y * lax.rsqrt(ms + EPS) * g_ref[...]).astype(o_ref.dtype)
        return carry

    lax.fori_loop(0, seq // ATT_FIN, combine, 0)


def _attention(qkv, attn_norm_row, slopes, batch, seq):
    m = qkv.shape[0]
    pairs = ATT_HEADS // 2
    w = 2 * HEAD_DIM
    nbr = len(DILATED_BRANCHES)
    return pl.pallas_call(
        functools.partial(_attn_body, seq=seq),
        grid=(batch, pairs),
        in_specs=[
            pl.BlockSpec(memory_space=pltpu.SMEM),
            pl.BlockSpec((seq, w), lambda b, p: (b, p)),
            pl.BlockSpec((seq, w), lambda b, p: (b, pairs + p)),
            pl.BlockSpec((seq, w), lambda b, p: (b, 2 * pairs + p)),
            pl.BlockSpec((1, w), lambda b, p: (0, p)),
            pl.BlockSpec((w, w), lambda b, p: (0, 0)),
        ],
        out_specs=pl.BlockSpec((seq, w), lambda b, p: (b, p)),
        out_shape=jax.ShapeDtypeStruct((m, ATT_WIDTH), bf16),
        scratch_shapes=[
            pltpu.VMEM((nbr, seq, w), f32),
            pltpu.VMEM((nbr, seq, w), f32),
            pltpu.VMEM((nbr, seq, w), f32),
            pltpu.VMEM((nbr, len(ATT_OFFSETS), 2 * ATT_TQ, ATT_TK), f32),
            pltpu.VMEM((3, seq, w), f32),
        ],
        compiler_params=_cparams(2),
        name="dilated_attention",
    )(slopes, qkv, qkv, qkv, attn_norm_row, _group_ones(w, HEAD_DIM))


SSD_L = SSD_CHUNK
SSD_GW = SSD_WIDTH // SSD_GROUPS
SSD_E = SSD_HEADS // SSD_GROUPS
SSD_XC = SSD_GW + 2 * SSD_STATE
SSD_HALO = 16


def _ssd_body(z_ref, xs_ref, b_ref, c_ref, dt_ref, cw_ref, cb_ref, dtb_ref, alog_ref,
              dsk_ref, ng_ref, expand_ref, o_ref, stage_ref, xc_ref, y_ref, st_ref, *, seq):
    n_chunks = seq // SSD_L
    L = SSD_L
    row_i = lax.broadcasted_iota(jnp.int32, (L, L), 0)
    col_i = lax.broadcasted_iota(jnp.int32, (L, L), 1)
    lower = row_i >= col_i
    upper = row_i <= col_i
    lower_b = lower.astype(bf16)
    upper_b = upper.astype(bf16)
    lane8 = lax.broadcasted_iota(jnp.int32, (1, L), 1)
    fwd_lane = lane8 < SSD_E
    sub8 = lax.broadcasted_iota(jnp.int32, (L, 1), 0)
    fwd_row = sub8 < SSD_E
    a_row = -jnp.exp(alog_ref[...])
    expand = expand_ref[...]
    lane_blk = lax.broadcasted_iota(jnp.int32, (1, SSD_GW), 1) // 64

    def conv_chunk(c):
        r0 = pl.multiple_of(c * L, L)
        rp = pl.multiple_of(jnp.maximum(r0 - SSD_HALO, 0), SSD_HALO)
        rn = pl.multiple_of(jnp.minimum(r0 + L, seq - SSD_HALO), SSD_HALO)
        keep_p = jnp.where(c > 0, 1.0, 0.0)
        keep_n = jnp.where(c < n_chunks - 1, 1.0, 0.0)
        slabs = []
        sl = 0
        for ref, width in ((xs_ref, SSD_GW), (b_ref, SSD_STATE), (c_ref, SSD_STATE)):
            for l0 in range(0, width, LANES):
                cs = slice(sl * LANES, (sl + 1) * LANES)
                cur = ref[pl.ds(r0, L), l0:l0 + LANES].astype(f32)
                stage_ref[sl, pl.ds(0, SSD_HALO, stride=2), :] = (
                    ref[pl.ds(rp, SSD_HALO), l0:l0 + LANES].astype(f32) * keep_p)
                stage_ref[sl, pl.ds(2 * SSD_HALO, L, stride=2), :] = cur
                stage_ref[sl, pl.ds(2 * (SSD_HALO + L), SSD_HALO, stride=2), :] = (
                    ref[pl.ds(rn, SSD_HALO), l0:l0 + LANES].astype(f32) * keep_n)
                acc = cb_ref[:, cs] + cw_ref[SSD_CONV // 2:SSD_CONV // 2 + 1, cs] * cur
                for kk in range(SSD_CONV):
                    if kk != SSD_CONV // 2:
                        o0 = SSD_HALO - SSD_CONV // 2 + kk
                        acc = acc + cw_ref[kk:kk + 1, cs] * stage_ref[sl, pl.ds(2 * o0, L, stride=2), :]
                slabs.append(acc)
                sl += 1
        return _silu(jnp.concatenate(slabs, axis=-1))

    def dt_terms(c):
        r0 = pl.multiple_of(c * L, L)
        dtv = jax.nn.softplus(dt_ref[pl.ds(r0, L), :] + dtb_ref[...])
        adt = dtv * a_row
        parts = _split3(adt)
        cum = jnp.where(fwd_lane, _parts_dot(lower_b, parts), _parts_dot(upper_b, parts))
        tot = jnp.sum(adt, axis=0, keepdims=True)
        return dtv, adt, cum, tot

    def expand_lanes(a):
        return _dot_parts(_split2(a), expand)

    st_ref[...] = jnp.zeros_like(st_ref)

    def pass_a(c, carry):
        r0 = pl.multiple_of(c * L, L)
        xc = conv_chunk(c)
        xc_ref[pl.ds(r0, L), :] = xc.astype(bf16)
        xs = xc[:, 0:SSD_GW]
        bm = xc[:, SSD_GW:SSD_GW + SSD_STATE].astype(bf16)
        cm = xc[:, SSD_GW + SSD_STATE:SSD_XC].astype(bf16)
        dtv, adt, cum, tot = dt_terms(c)
        adt_t = adt.T
        parts_t = _split3(adt_t)
        cum_t = jnp.where(fwd_row, _dot_parts(parts_t, upper_b), _dot_parts(parts_t, lower_b))
        gram = lax.dot_general(cm, bm, _NT, preferred_element_type=f32)
        dt_x = expand_lanes(dtv)
        x_cat = jnp.concatenate(
            [(xs * dt_x[:, 0:SSD_GW]).astype(bf16), (xs * dt_x[:, SSD_GW:]).astype(bf16)], axis=0)
        m_rows = []
        for e in range(SSD_E):
            df = jnp.exp(jnp.where(lower, cum[:, e:e + 1] - cum_t[e:e + 1, :], NEG))
            db = jnp.exp(jnp.where(upper, cum[:, SSD_E + e:SSD_E + e + 1]
                                   - cum_t[SSD_E + e:SSD_E + e + 1, :], NEG))
            m_rows.append(jnp.concatenate([(gram * df).astype(bf16), (gram * db).astype(bf16)], axis=1))
        y_all = _dot(jnp.concatenate(m_rows, axis=0), x_cat)
        y = jnp.zeros((L, SSD_GW), f32)
        for e in range(SSD_E):
            y = jnp.where(lane_blk == e, y_all[e * L:(e + 1) * L, :], y)
        ecum_x = expand_lanes(jnp.exp(cum))
        state = st_ref[...]
        y = y + _dot(cm, state.astype(bf16)) * ecum_x[:, 0:SSD_GW]
        y_ref[pl.ds(r0, L), :] = y
        w_x = expand_lanes(jnp.exp(tot - cum) * dtv)
        xin = (xs * w_x[:, 0:SSD_GW]).astype(bf16)
        dec = expand_lanes(jnp.broadcast_to(jnp.exp(tot), (8, L)))[0:1, 0:SSD_GW]
        st_ref[...] = state * dec + lax.dot_general(
            bm, xin, (((0,), (0,)), ((), ())), preferred_element_type=f32)
        return carry

    lax.fori_loop(0, n_chunks, pass_a, 0)

    st_ref[...] = jnp.zeros_like(st_ref)

    def pass_b(i, carry):
        c = n_chunks - 1 - i
        r0 = pl.multiple_of(c * L, L)
        xcb = xc_ref[pl.ds(r0, L), :]
        xs = xcb[:, 0:SSD_GW].astype(f32)
        bm = xcb[:, SSD_GW:SSD_GW + SSD_STATE]
        cm = xcb[:, SSD_GW + SSD_STATE:SSD_XC]
        dtv, adt, cum, tot = dt_terms(c)
        ecum_x = expand_lanes(jnp.exp(cum))
        state = st_ref[...]
        y = y_ref[pl.ds(r0, L), :] + _dot(cm, state.astype(bf16)) * ecum_x[:, SSD_GW:]
        w_x = expand_lanes(jnp.exp(tot - cum) * dtv)
        xin = (xs * w_x[:, SSD_GW:]).astype(bf16)
        dec = expand_lanes(jnp.broadcast_to(jnp.exp(tot), (8, L)))[0:1, SSD_GW:]
        st_ref[...] = state * dec + lax.dot_general(
            bm, xin, (((0,), (0,)), ((), ())), preferred_element_type=f32)
        y = y + dsk_ref[...] * xs
        y = y * _silu(z_ref[pl.ds(r0, L), :].astype(f32))
        ms = jnp.mean(y * y, axis=-1, keepdims=True)
        o_ref[pl.ds(r0, L), :] = (y * lax.rsqrt(ms + EPS) * ng_ref[...]).astype(o_ref.dtype)
        return carry

    lax.fori_loop(0, n_chunks, pass_b, 0)


def _ssd(pr, dt, conv_w_g, conv_b_g, dtb_row, alog_row, dskip_row, norm_row, batch, seq):
    m = pr.shape[0]
    expand = np.zeros((SSD_L, 2 * SSD_GW), np.float32)
    for e in range(SSD_E):
        expand[e, 64 * e:64 * (e + 1)] = 1.0
        expand[SSD_E + e, SSD_GW + 64 * e:SSD_GW + 64 * (e + 1)] = 1.0
    expand = jnp.asarray(expand, dtype=bf16)
    return pl.pallas_call(
        functools.partial(_ssd_body, seq=seq),
        grid=(batch, SSD_GROUPS),
        in_specs=[
            pl.BlockSpec((seq, SSD_GW), lambda b, g: (b, g)),
            pl.BlockSpec((seq, SSD_GW), lambda b, g: (b, 2 + g)),
            pl.BlockSpec((seq, SSD_STATE), lambda b, g: (b, 8 + g)),
            pl.BlockSpec((seq, SSD_STATE), lambda b, g: (b, 10 + g)),
            pl.BlockSpec((seq, 128), lambda b, g: (b, g)),
            pl.BlockSpec((None, SSD_CONV, SSD_XC), lambda b, g: (g, 0, 0)),
            pl.BlockSpec((None, 1, SSD_XC), lambda b, g: (g, 0, 0)),
            pl.BlockSpec((None, 1, 128), lambda b, g: (g, 0, 0)),
            pl.BlockSpec((None, 1, 128), lambda b, g: (g, 0, 0)),
            pl.BlockSpec((None, 1, SSD_GW), lambda b, g: (g, 0, 0)),
            pl.BlockSpec((None, 1, SSD_GW), lambda b, g: (g, 0, 0)),
            pl.BlockSpec((SSD_L, 2 * SSD_GW), lambda b, g: (0, 0)),
        ],
        out_specs=pl.BlockSpec((seq, SSD_GW), lambda b, g: (b, g)),
        out_shape=jax.ShapeDtypeStruct((m, SSD_WIDTH), bf16),
        scratch_shapes=[
            pltpu.VMEM((SSD_XC // LANES, 2 * (SSD_L + 2 * SSD_HALO), LANES), f32),
            pltpu.VMEM((seq, SSD_XC), bf16),
            pltpu.VMEM((seq, SSD_GW), f32),
            pltpu.VMEM((SSD_STATE, SSD_GW), f32),
        ],
        compiler_params=_cparams(2),
        name="ssd",
    )(pr, pr, pr, pr, dt, conv_w_g, conv_b_g, dtb_row, alog_row, dskip_row, norm_row, expand)


OUT_TM = 512
OUT_HALO = 16


def _outproj_body(att_ref, ssm_ref, gb_ref, gcp_ref, gc_ref, gcn_ref, hcp_ref, hc_ref, hcn_ref,
                  x_ref, w_ref, cw_ref, cb_ref, ng_ref, gmat_ref, o_ref, stage_ref, *, tiles_per_seq):
    i = pl.program_id(0)
    keep_p = jnp.where(i % tiles_per_seq > 0, 1.0, 0.0)
    keep_n = jnp.where(i % tiles_per_seq < tiles_per_seq - 1, 1.0, 0.0)
    tm = OUT_TM
    h = OUT_HALO
    slabs = []
    for sl in range(CONV_WIDTH // LANES):
        cs = slice(sl * LANES, (sl + 1) * LANES)
        cur = gc_ref[:, cs].astype(f32) * hc_ref[:, cs].astype(f32)
        stage_ref[sl, pl.ds(0, h, stride=2), :] = gcp_ref[:, cs].astype(f32) * hcp_ref[:, cs].astype(f32) * keep_p
        stage_ref[sl, pl.ds(2 * h, tm, stride=2), :] = cur
        stage_ref[sl, pl.ds(2 * (h + tm), h, stride=2), :] = (
            gcn_ref[:, cs].astype(f32) * hcn_ref[:, cs].astype(f32) * keep_n)
        slabs.append(cb_ref[:, cs] + cw_ref[0:1, cs] * stage_ref[sl, pl.ds(2 * (h - 1), tm, stride=2), :]
                     + cw_ref[1:2, cs] * cur
                     + cw_ref[2:3, cs] * stage_ref[sl, pl.ds(2 * (h + 1), tm, stride=2), :])
    y = gb_ref[...].astype(f32) * jnp.concatenate(slabs, axis=-1)
    ms = _dot_parts(_split2(y * y), gmat_ref[...]) * (1.0 / (CONV_WIDTH // CONV_GROUPS))
    sc = (y * lax.rsqrt(ms + EPS) * ng_ref[...]).astype(bf16)
    mix = jnp.concatenate([att_ref[...], ssm_ref[...], sc], axis=-1)
    o_ref[...] = x_ref[...] + _dot(mix, w_ref[...])


def _outproj(att, ssm, pr, x2, w_out, sc_w, sc_b, sc_g, seq):
    m = x2.shape[0]
    tm, h = OUT_TM, OUT_HALO
    per = tm // h
    nb = m // h
    const = lambda i: (0, 0)
    prev = lambda col: (lambda i: (jnp.maximum(i * per - 1, 0), col))
    nxt = lambda col: (lambda i: (jnp.minimum((i + 1) * per, nb - 1), col))
    return pl.pallas_call(
        functools.partial(_outproj_body, tiles_per_seq=seq // tm),
        grid=(m // tm,),
        in_specs=[
            pl.BlockSpec((tm, ATT_WIDTH), lambda i: (i, 0)),
            pl.BlockSpec((tm, SSD_WIDTH), lambda i: (i, 0)),
            pl.BlockSpec((tm, CONV_WIDTH), lambda i: (i, 3)),
            pl.BlockSpec((h, CONV_WIDTH), prev(4)),
            pl.BlockSpec((tm, CONV_WIDTH), lambda i: (i, 4)),
            pl.BlockSpec((h, CONV_WIDTH), nxt(4)),
            pl.BlockSpec((h, CONV_WIDTH), prev(5)),
            pl.BlockSpec((tm, CONV_WIDTH), lambda i: (i, 5)),
            pl.BlockSpec((h, CONV_WIDTH), nxt(5)),
            pl.BlockSpec((tm, D_MODEL), lambda i: (i, 0)),
            pl.BlockSpec((D_MIX, D_MODEL), const, pipeline_mode=pl.Buffered(1)),
            pl.BlockSpec((3, CONV_WIDTH), const),
            pl.BlockSpec((1, CONV_WIDTH), const),
            pl.BlockSpec((1, CONV_WIDTH), const),
            pl.BlockSpec((CONV_WIDTH, CONV_WIDTH), const),
        ],
        out_specs=pl.BlockSpec((tm, D_MODEL), lambda i: (i, 0)),
        out_shape=jax.ShapeDtypeStruct((m, D_MODEL), f32),
        scratch_shapes=[pltpu.VMEM((CONV_WIDTH // LANES, 2 * (tm + 2 * h), LANES), f32)],
        compiler_params=_cparams(1),
        name="outproj",
    )(att, ssm, pr, pr, pr, pr, pr, pr, pr, x2, w_out, sc_w, sc_b, sc_g,
      _group_ones(CONV_WIDTH, CONV_WIDTH // CONV_GROUPS))


FFN_TM = 512
FFN_HALO = 8
FFN_CHUNK = 256


def _ffn_body(xp_ref, x_ref, xn_ref, g_ref, wup_ref, cw_ref, cb_ref, wdn_ref, fg_ref, o_ref,
              hn_ref, u_ref, act_ref, *, tiles_per_seq, final_norm):
    i = pl.program_id(0)
    tm, h = FFN_TM, FFN_HALO
    keep_p = jnp.where(i % tiles_per_seq > 0, 1.0, 0.0)
    keep_n = jnp.where(i % tiles_per_seq < tiles_per_seq - 1, 1.0, 0.0)
    xall = jnp.concatenate([xp_ref[...] * keep_p, x_ref[...], xn_ref[...] * keep_n], axis=0)
    ms = jnp.mean(xall * xall, axis=-1, keepdims=True)
    hn_ref[...] = ((xall * lax.rsqrt(ms + EPS)) * g_ref[...]).astype(bf16)
    rows = tm + 2 * h
    for c0 in range(0, D_FF, FFN_CHUNK):
        halves = []
        for part in range(2):
            col = part * D_FF + c0
            u = _dot(hn_ref[...], wup_ref[:, col:col + FFN_CHUNK])
            slabs = []
            for sl in range(FFN_CHUNK // LANES):
                u_ref[part, sl, pl.ds(0, rows, stride=2), :] = u[:, sl * LANES:(sl + 1) * LANES]
                cs = slice(col + sl * LANES, col + (sl + 1) * LANES)
                slabs.append(cb_ref[:, cs]
                             + cw_ref[0:1, cs] * u_ref[part, sl, pl.ds(2 * (h - 1), tm, stride=2), :]
                             + cw_ref[1:2, cs] * u[h:h + tm, sl * LANES:(sl + 1) * LANES]
                             + cw_ref[2:3, cs] * u_ref[part, sl, pl.ds(2 * (h + 1), tm, stride=2), :])
            halves.append(jnp.concatenate(slabs, axis=-1))
        act_ref[:, c0:c0 + FFN_CHUNK] = (_silu(halves[0]) * halves[1]).astype(bf16)
    out = x_ref[...] + _dot(act_ref[...], wdn_ref[...])
    if final_norm:
        ms2 = jnp.mean(out * out, axis=-1, keepdims=True)
        out = (out * lax.rsqrt(ms2 + EPS)) * fg_ref[...]
    o_ref[...] = out


def _ffn(x2, g, w_up, conv_w, conv_b, w_down, final_g, seq, final_norm):
    m = x2.shape[0]
    tm, h = FFN_TM, FFN_HALO
    per = tm // h
    nb = m // h
    const = lambda i: (0, 0)
    return pl.pallas_call(
        functools.partial(_ffn_body, tiles_per_seq=seq // tm, final_norm=final_norm),
        grid=(m // tm,),
        in_specs=[
            pl.BlockSpec((h, D_MODEL), lambda i: (jnp.maximum(i * per - 1, 0), 0)),
            pl.BlockSpec((tm, D_MODEL), lambda i: (i, 0)),
            pl.BlockSpec((h, D_MODEL), lambda i: (jnp.minimum((i + 1) * per, nb - 1), 0)),
            pl.BlockSpec((1, D_MODEL), const),
            pl.BlockSpec((D_MODEL, 2 * D_FF), const, pipeline_mode=pl.Buffered(1)),
            pl.BlockSpec((3, 2 * D_FF), const),
            pl.BlockSpec((1, 2 * D_FF), const),
            pl.BlockSpec((D_FF, D_MODEL), const, pipeline_mode=pl.Buffered(1)),
            pl.BlockSpec((1, D_MODEL), const),
        ],
        out_specs=pl.BlockSpec((tm, D_MODEL), lambda i: (i, 0)),
        out_shape=jax.ShapeDtypeStruct((m, D_MODEL), f32),
        scratch_shapes=[
            pltpu.VMEM((tm + 2 * h, D_MODEL), bf16),
            pltpu.VMEM((2, FFN_CHUNK // LANES, 2 * (tm + 2 * h), LANES), f32),
            pltpu.VMEM((tm, D_FF), bf16),
        ],
        compiler_params=_cparams(1),
        name="convffn",
    )(x2, x2, x2, g, w_up, conv_w, conv_b, w_down, final_g)


def _layer_params(i, w_in, ssd_conv_w, ssd_conv_b, ssd_dt_bias, ssd_a_log, ssd_d, ssd_norm):
    w = w_in[i]
    dt0 = 3 * ATT_WIDTH + SSD_WIDTH + SSD_WIDTH + 2 * SSD_GROUPS * SSD_STATE
    w_main = jnp.concatenate([w[:, :dt0], w[:, dt0 + 2 * SSD_HEADS:]], axis=1).astype(bf16)
    w_dt_cols = w[:, dt0:dt0 + 2 * SSD_HEADS]

    def per_group_lanes(v2):
        v = v2.reshape(2, SSD_GROUPS, SSD_E).transpose(1, 0, 2).reshape(SSD_GROUPS, 1, 2 * SSD_E)
        return jnp.pad(v, ((0, 0), (0, 0), (0, 128 - 2 * SSD_E)))

    wd = w_dt_cols.reshape(D_MODEL, 2, SSD_GROUPS, SSD_E).transpose(0, 2, 1, 3)
    wd = wd.reshape(D_MODEL, SSD_GROUPS, 2 * SSD_E)
    w_dt = jnp.pad(wd, ((0, 0), (0, 0), (0, 128 - 2 * SSD_E))).reshape(D_MODEL, DT_W).astype(bf16)

    cw, cb = ssd_conv_w[i], ssd_conv_b[i]
    def group_channels(a):
        outs = []
        for g in range(SSD_GROUPS):
            outs.append(jnp.concatenate([
                a[..., g * SSD_GW:(g + 1) * SSD_GW],
                a[..., SSD_WIDTH + g * SSD_STATE:SSD_WIDTH + (g + 1) * SSD_STATE],
                a[..., SSD_WIDTH + (SSD_GROUPS + g) * SSD_STATE:SSD_WIDTH + (SSD_GROUPS + g + 1) * SSD_STATE],
            ], axis=-1))
        return jnp.stack(outs, axis=0)

    conv_w_g = group_channels(cw)
    conv_b_g = group_channels(cb[None, :])
    dtb_row = per_group_lanes(ssd_dt_bias[i])
    alog_row = per_group_lanes(ssd_a_log[i])
    dskip_row = jnp.repeat(ssd_d[i], HEAD_DIM).reshape(SSD_GROUPS, 1, SSD_GW)
    norm_row = ssd_norm[i].reshape(SSD_GROUPS, 1, SSD_GW)
    return w_main, w_dt, conv_w_g, conv_b_g, dtb_row, alog_row, dskip_row, norm_row


def kernel(x, mix_norm, w_in, ssd_conv_w, ssd_conv_b, ssd_dt_bias, ssd_a_log, ssd_d, ssd_norm, sc_conv_w, sc_conv_b, attn_norm, sc_norm, w_out, ffn_norm, w_up, ffn_conv_w, ffn_conv_b, w_down, final_norm):
    batch, seq, d = x.shape
    depth = w_in.shape[0]
    x2 = x.reshape(batch * seq, d)
    slopes = jnp.asarray(2.0 ** (-8.0 * (np.arange(ATT_HEADS) + 1.0) / ATT_HEADS), dtype=f32)
    for i in range(depth):
        (w_main, w_dt, conv_w_g, conv_b_g, dtb_row, alog_row, dskip_row, norm_row) = _layer_params(
            i, w_in, ssd_conv_w, ssd_conv_b, ssd_dt_bias, ssd_a_log, ssd_d, ssd_norm)
        qkv, pr, dt = _inproj(x2, mix_norm[i][None, :], w_main, w_dt)
        att = _attention(qkv, attn_norm[i][None, :], slopes, batch, seq)
        ssm = _ssd(pr, dt, conv_w_g, conv_b_g, dtb_row, alog_row, dskip_row, norm_row, batch, seq)
        x2 = _outproj(att, ssm, pr, x2, w_out[i].astype(bf16), sc_conv_w[i], sc_conv_b[i][None, :],
                      sc_norm[i][None, :], seq)
        x2 = _ffn(x2, ffn_norm[i][None, :], w_up[i].astype(bf16), ffn_conv_w[i],
                  ffn_conv_b[i][None, :], w_down[i].astype(bf16), final_norm[None, :], seq,
                  final_norm=(i == depth - 1))
    return x2.reshape(batch, seq, d)
```

```python
import functools

import numpy as np
import jax
import jax.numpy as jnp
from jax import lax
from jax.experimental import pallas as pl
from jax.experimental.pallas import tpu as pltpu

f32 = jnp.float32
bf16 = jnp.bfloat16

EPS = 1e-6
NEG = -1e30

D_MODEL = 1024
HEAD_DIM = 64
ATT_HEADS = 8
ATT_WIDTH = 512
DILATED_BRANCHES = ((128, 1), (512, 4), (2048, 16))
ATT_HALF = 64
SSD_HEADS = 8
SSD_WIDTH = 512
SSD_GROUPS = 2
SSD_STATE = 128
SSD_CONV = 5
SSD_CHUNK = 128
CONV_WIDTH = 512
CONV_GROUPS = 8
D_MIX = 1536
D_FF = 2816
D_IN = 4624

VMEM_LIMIT = 56 * 1024 * 1024
LANES = 128

_NT = (((1,), (1,)), ((), ()))


def _cparams(n_axes):
    return pltpu.CompilerParams(
        dimension_semantics=("arbitrary",) * n_axes, vmem_limit_bytes=VMEM_LIMIT)


def _split2(x):
    hi = x.astype(bf16)
    lo = (x - hi.astype(f32)).astype(bf16)
    return hi, lo


def _split3(x):
    h1 = x.astype(bf16)
    r1 = x - h1.astype(f32)
    h2 = r1.astype(bf16)
    h3 = (r1 - h2.astype(f32)).astype(bf16)
    return h1, h2, h3


def _dot(a, b):
    return jnp.dot(a, b, preferred_element_type=f32)


def _dot_parts(parts, mat):
    acc = _dot(parts[0], mat)
    for p in parts[1:]:
        acc = acc + _dot(p, mat)
    return acc


def _parts_dot(mat, parts):
    acc = _dot(mat, parts[0])
    for p in parts[1:]:
        acc = acc + _dot(mat, p)
    return acc


def _silu(x):
    return x * jax.nn.sigmoid(x)


def _group_ones(width, group):
    idx = np.arange(width) // group
    return jnp.asarray(idx[:, None] == idx[None, :], dtype=bf16)


IN_TM = 512
IN_CHUNK = 512
QKV_W = 3 * ATT_WIDTH
PR_W = 3072
DT_W = 256
XBC_LO, XBC_HI = 512, 1536
IN_HALO = 8


def _inproj_body(xp_ref, x_ref, xn_ref, g_ref, w_ref, wdt_ref, cw_ref, cb_ref,
                 qkv_ref, pr_ref, dt_ref, hn_ref, stage_ref, *, tiles_per_seq):
    i = pl.program_id(0)
    tm, h = IN_TM, IN_HALO
    rows = tm + 2 * h
    keep_p = jnp.where(i % tiles_per_seq > 0, 1.0, 0.0)
    keep_n = jnp.where(i % tiles_per_seq < tiles_per_seq - 1, 1.0, 0.0)
    x = x_ref[...]
    ms = jnp.mean(x * x, axis=-1, keepdims=True)
    hn = ((x * lax.rsqrt(ms + EPS)) * g_ref[...]).astype(bf16)
    for c0 in range(0, QKV_W, IN_CHUNK):
        qkv_ref[:, c0:c0 + IN_CHUNK] = _dot(hn, w_ref[:, c0:c0 + IN_CHUNK])
    for c0 in list(range(0, XBC_LO, IN_CHUNK)) + list(range(XBC_HI, PR_W, IN_CHUNK)):
        pr_ref[:, c0:c0 + IN_CHUNK] = _dot(
            hn, w_ref[:, QKV_W + c0:QKV_W + c0 + IN_CHUNK]).astype(bf16)
    dt_ref[...] = _dot(hn, wdt_ref[...])
    halo = jnp.concatenate([xp_ref[...] * keep_p, xn_ref[...] * keep_n], axis=0)
    msh = jnp.mean(halo * halo, axis=-1, keepdims=True)
    hh = ((halo * lax.rsqrt(msh + EPS)) * g_ref[...]).astype(bf16)
    hn_ref[0:2 * h, :] = hh
    hn_ref[2 * h:, :] = hn
    half = SSD_CONV // 2
    for c0 in range(XBC_LO, XBC_HI, IN_CHUNK):
        u = _dot(hn_ref[...], w_ref[:, QKV_W + c0:QKV_W + c0 + IN_CHUNK])
        slabs = []
        for sl in range(IN_CHUNK // LANES):
            ls = slice(sl * LANES, (sl + 1) * LANES)
            cs = slice(c0 - XBC_LO + sl * LANES, c0 - XBC_LO + (sl + 1) * LANES)
            cur = u[2 * h:, ls]
            stage_ref[sl, pl.ds(0, h, stride=2), :] = u[0:h, ls]
            stage_ref[sl, pl.ds(2 * h, tm, stride=2), :] = cur
            stage_ref[sl, pl.ds(2 * (h + tm), h, stride=2), :] = u[h:2 * h, ls]
            acc = cb_ref[:, cs] + cw_ref[half:half + 1, cs] * cur
            for kk in range(SSD_CONV):
                if kk != half:
                    acc = acc + cw_ref[kk:kk + 1, cs] * stage_ref[sl, pl.ds(2 * (h - half + kk), tm, stride=2), :]
            slabs.append(acc)
        pr_ref[:, c0:c0 + IN_CHUNK] = _silu(jnp.concatenate(slabs, axis=-1)).astype(bf16)


def _inproj(x2, g, w_main, w_dt, conv_w, conv_b, seq):
    m = x2.shape[0]
    tm, h = IN_TM, IN_HALO
    per = tm // h
    nb = m // h
    const = lambda i: (0, 0)
    return pl.pallas_call(
        functools.partial(_inproj_body, tiles_per_seq=seq // tm),
        grid=(m // IN_TM,),
        in_specs=[
            pl.BlockSpec((h, D_MODEL), lambda i: (jnp.maximum(i * per - 1, 0), 0)),
            pl.BlockSpec((IN_TM, D_MODEL), lambda i: (i, 0)),
            pl.BlockSpec((h, D_MODEL), lambda i: (jnp.minimum((i + 1) * per, nb - 1), 0)),
            pl.BlockSpec((1, D_MODEL), const),
            pl.BlockSpec((D_MODEL, QKV_W + PR_W), const, pipeline_mode=pl.Buffered(1)),
            pl.BlockSpec((D_MODEL, DT_W), const, pipeline_mode=pl.Buffered(1)),
            pl.BlockSpec((SSD_CONV, XBC_HI - XBC_LO), const),
            pl.BlockSpec((1, XBC_HI - XBC_LO), const),
        ],
        out_specs=[
            pl.BlockSpec((IN_TM, QKV_W), lambda i: (i, 0)),
            pl.BlockSpec((IN_TM, PR_W), lambda i: (i, 0)),
            pl.BlockSpec((IN_TM, DT_W), lambda i: (i, 0)),
        ],
        out_shape=[
            jax.ShapeDtypeStruct((m, QKV_W), f32),
            jax.ShapeDtypeStruct((m, PR_W), bf16),
            jax.ShapeDtypeStruct((m, DT_W), f32),
        ],
        scratch_shapes=[
            pltpu.VMEM((tm + 2 * h, D_MODEL), bf16),
            pltpu.VMEM((IN_CHUNK // LANES, 2 * (tm + 2 * h), LANES), f32),
        ],
        compiler_params=_cparams(1),
        name="inproj",
    )(x2, x2, x2, g, w_main, w_dt, conv_w, conv_b)


ATT_TQ = 128
ATT_TK = ATT_TQ + 2 * ATT_HALF
ATT_FIN = 512
ATT_UNROLL = 8
ATT_OFFSETS = (0, -ATT_HALF, -2 * ATT_HALF)
ATT_PRE = 4


def _attn_body(slopes_ref, q_ref, k_ref, v_ref, g_ref, gmat_ref, o_ref,
               acc_ref, m_ref, l_ref, bias_ref, d4_ref, *, seq):
    pair = pl.program_id(1)
    lane = lax.broadcasted_iota(jnp.int32, (1, 2 * HEAD_DIM), 1)
    head0 = lane < HEAD_DIM
    qmask0 = jnp.where(head0, HEAD_DIM ** -0.5, 0.0)
    qmask1 = jnp.where(head0, 0.0, HEAD_DIM ** -0.5)

    row = lax.broadcasted_iota(jnp.int32, (2 * ATT_TQ, ATT_TK), 0)
    col = lax.broadcasted_iota(jnp.int32, (2 * ATT_TQ, ATT_TK), 1)
    rel0 = col - (row & (ATT_TQ - 1))
    slope_rows = jnp.where(row < ATT_TQ, slopes_ref[2 * pair], slopes_ref[2 * pair + 1])
    for bi, (_, r) in enumerate(DILATED_BRANCHES):
        for vi, off in enumerate(ATT_OFFSETS):
            dist = jnp.abs(rel0 + off)
            bias_ref[bi, vi] = jnp.where(dist <= ATT_HALF, dist.astype(f32) * (slope_rows * (-float(r))), NEG)

    pre_len = seq // ATT_PRE

    def deinterleave(j, carry):
        for a, ref in enumerate((q_ref, k_ref, v_ref)):
            for c in range(ATT_PRE):
                d4_ref[a, pl.ds(c * pre_len + pl.multiple_of(j * ATT_FIN, ATT_FIN), ATT_FIN), :] = (
                    ref[pl.ds(c + ATT_PRE * pl.multiple_of(j * ATT_FIN, ATT_FIN), ATT_FIN, stride=ATT_PRE), :])
        return carry

    lax.fori_loop(0, pre_len // ATT_FIN, deinterleave, 0)

    for bi, (_, r) in enumerate(DILATED_BRANCHES):
        sub_len = seq // r
        tiles_per_class = sub_len // ATT_TQ

        def scores(t, r=r, sub_len=sub_len, tiles_per_class=tiles_per_class):
            cls = t // tiles_per_class
            t0 = (t % tiles_per_class) * ATT_TQ
            k0 = jnp.clip(t0 - ATT_HALF, 0, sub_len - ATT_TK)
            variant = jnp.where(t0 == 0, 0, jnp.where(t0 == sub_len - ATT_TQ, 2, 1))
            q_rows = pl.ds(cls + r * t0, ATT_TQ, stride=r)
            if r > ATT_PRE:
                base = (cls % ATT_PRE) * (seq // ATT_PRE) + cls // ATT_PRE
                q = d4_ref[0, pl.ds(base + (r // ATT_PRE) * t0, ATT_TQ, stride=r // ATT_PRE), :]
                k_rows = pl.ds(base + (r // ATT_PRE) * k0, ATT_TK, stride=r // ATT_PRE)
                k = d4_ref[1, k_rows, :]
            else:
                q = q_ref[q_rows, :]
                k_rows = pl.ds(cls + r * k0, ATT_TK, stride=r)
                k = k_ref[k_rows, :]
            q2 = jnp.concatenate([q * qmask0, q * qmask1], axis=0).astype(bf16)
            s = lax.dot_general(q2, k.astype(bf16), _NT, preferred_element_type=f32)
            return q_rows, k_rows, variant, s

        def softmax(unit, bi=bi):
            q_rows, k_rows, variant, s = unit
            s = s + bias_ref[bi, variant]
            m = jnp.max(s, axis=-1, keepdims=True)
            return q_rows, k_rows, jnp.exp(s - m).astype(bf16), m

        def values(unit, bi=bi, r=r):
            q_rows, k_rows, p, m = unit
            v = d4_ref[2, k_rows, :] if r > ATT_PRE else v_ref[k_rows, :]
            o0 = _dot(p[0:ATT_TQ], jnp.where(head0, v, 1.0).astype(bf16))
            o1 = _dot(p[ATT_TQ:], jnp.where(head0, 1.0, v).astype(bf16))
            acc_ref[bi, q_rows, :] = jnp.where(head0, o0, o1)
            l_ref[bi, q_rows, :] = jnp.where(head0, o1, o0)
            m_ref[bi, q_rows, :] = jnp.where(head0, m[0:ATT_TQ], m[ATT_TQ:])

        def tile_group(g, carry):
            st_s, st_p = {}, {}
            for step in range(ATT_UNROLL + 2):
                if step < ATT_UNROLL:
                    st_s[step] = scores(g * ATT_UNROLL + step)
                if 0 <= step - 1 < ATT_UNROLL:
                    st_p[step - 1] = softmax(st_s.pop(step - 1))
                if 0 <= step - 2 < ATT_UNROLL:
                    values(st_p.pop(step - 2))
            return carry

        lax.fori_loop(0, seq // (ATT_TQ * ATT_UNROLL), tile_group, 0)

    gmat = gmat_ref[...]

    def combine(t, carry):
        rows = pl.ds(pl.multiple_of(t * ATT_FIN, ATT_FIN), ATT_FIN)
        ms_ = [m_ref[b, rows, :] for b in range(len(DILATED_BRANCHES))]
        mx = functools.reduce(jnp.maximum, ms_)
        num, den = None, None
        for b, mb in enumerate(ms_):
            e = jnp.exp(mb - mx)
            nb = e * acc_ref[b, rows, :]
            db = e * pltpu.roll(l_ref[b, rows, :], HEAD_DIM, axis=1)
            num = nb if num is None else num + nb
            den = db if den is None else den + db
        y = num / den
        ms = _dot_parts(_split2(y * y), gmat) * (1.0 / HEAD_DIM)
        o_ref[rows, :] = (y * lax.rsqrt(ms + EPS) * g_ref[...]).astype(o_ref.dtype)
        return carry

    lax.fori_loop(0, seq // ATT_FIN, combine, 0)


def _attention(qkv, attn_norm_row, slopes, batch, seq):
    m = qkv.shape[0]
    pairs = ATT_HEADS // 2
    w = 2 * HEAD_DIM
    nbr = len(DILATED_BRANCHES)
    return pl.pallas_call(
        functools.partial(_attn_body, seq=seq),
        grid=(batch, pairs),
        in_specs=[
            pl.BlockSpec(memory_space=pltpu.SMEM),
            pl.BlockSpec((seq, w), lambda b, p: (b, p)),
            pl.BlockSpec((seq, w), lambda b, p: (b, pairs + p)),
            pl.BlockSpec((seq, w), lambda b, p: (b, 2 * pairs + p)),
            pl.BlockSpec((1, w), lambda b, p: (0, p)),
            pl.BlockSpec((w, w), lambda b, p: (0, 0)),
        ],
        out_specs=pl.BlockSpec((seq, w), lambda b, p: (b, p)),
        out_shape=jax.ShapeDtypeStruct((m, ATT_WIDTH), bf16),
        scratch_shapes=[
            pltpu.VMEM((nbr, seq, w), f32),
            pltpu.VMEM((nbr, seq, w), f32),
            pltpu.VMEM((nbr, seq, w), f32),
            pltpu.VMEM((nbr, len(ATT_OFFSETS), 2 * ATT_TQ, ATT_TK), f32),
            pltpu.VMEM((3, seq, w), f32),
        ],
        compiler_params=_cparams(2),
        name="dilated_attention",
    )(slopes, qkv, qkv, qkv, attn_norm_row, _group_ones(w, HEAD_DIM))


SSD_L = SSD_CHUNK
SSD_GW = SSD_WIDTH // SSD_GROUPS
SSD_E = SSD_HEADS // SSD_GROUPS
SSD_UA = 2
SSD_UB = 4


def _ssd_body(z_ref, xs_ref, b_ref, c_ref, dt_ref, dtb_ref, alog_ref,
              dsk_ref, ng_ref, expand_ref, o_ref,
              bmt_ref, y_ref, ecb_ref, xinb_ref, decb_ref, st_ref, *, seq):
    n_chunks = seq // SSD_L
    L = SSD_L
    row_i = lax.broadcasted_iota(jnp.int32, (L, L), 0)
    col_i = lax.broadcasted_iota(jnp.int32, (L, L), 1)
    lower = row_i >= col_i
    upper = row_i <= col_i
    lower_b = lower.astype(bf16)
    upper_b = upper.astype(bf16)
    lane8 = lax.broadcasted_iota(jnp.int32, (1, L), 1)
    fwd_lane = lane8 < SSD_E
    a_row = -jnp.exp(alog_ref[...])
    expand = expand_ref[...]
    lane_blk = lax.broadcasted_iota(jnp.int32, (1, SSD_GW), 1) // 64


    def stage1(c):
        rows = pl.ds(pl.multiple_of(c * L, L), L)
        d = dict(c=c, rows=rows, xs=xs_ref[rows, :].astype(f32), bm=b_ref[rows, :], cm=c_ref[rows, :])
        dtv = jax.nn.softplus(dt_ref[rows, :] + dtb_ref[...])
        adt = dtv * a_row
        parts = _split2(adt)
        d["cum"] = jnp.where(fwd_lane, _parts_dot(lower_b, parts), _parts_dot(upper_b, parts))
        d["tot"] = jnp.sum(adt, axis=0, keepdims=True)
        d["dtv"] = dtv
        return d

    def stage2(d):
        d["cum_t"] = d["cum"].T
        d["gram"] = lax.dot_general(d["cm"], d["bm"], _NT, preferred_element_type=f32)
        d["bm_t"] = d["bm"].T
        bmt_ref[d["c"]] = d["bm_t"]
        cum, tot, dtv = d["cum"], d["tot"], d["dtv"]
        coarse = _dot(jnp.concatenate([dtv, jnp.exp(tot - cum) * dtv], axis=0).astype(bf16), expand)
        fine = _dot_parts(_split2(jnp.concatenate(
            [jnp.exp(cum), jnp.broadcast_to(jnp.exp(tot), (8, L))], axis=0)), expand)
        d["dt_x"], d["w_x"] = coarse[0:L], coarse[L:]
        d["ecum_x"], d["dec"] = fine[0:L], fine[L:]
        return d

    def stage3(d):
        xs, cum, cum_t, gram = d["xs"], d["cum"], d["cum_t"], d["gram"]
        x_cat = jnp.concatenate(
            [(xs * d["dt_x"][:, 0:SSD_GW]).astype(bf16), (xs * d["dt_x"][:, SSD_GW:]).astype(bf16)], axis=0)
        m_rows = []
        for e in range(SSD_E):
            df = jnp.exp(jnp.where(lower, cum[:, e:e + 1] - cum_t[e:e + 1, :], NEG))
            db = jnp.exp(jnp.where(upper, cum[:, SSD_E + e:SSD_E + e + 1]
                                   - cum_t[SSD_E + e:SSD_E + e + 1, :], NEG))
            m_rows.append(jnp.concatenate([(gram * df).astype(bf16), (gram * db).astype(bf16)], axis=1))
        y_all = _dot(jnp.concatenate(m_rows, axis=0), x_cat)
        y = dsk_ref[...] * xs
        for e in range(SSD_E):
            y = jnp.where(lane_blk == e, y_all[e * L:(e + 1) * L, :] + y, y)
        d["y"] = y
        d["cs_f"] = _dot(d["bm_t"], (xs * d["w_x"][:, 0:SSD_GW]).astype(bf16))
        xinb_ref[d["rows"], :] = (xs * d["w_x"][:, SSD_GW:]).astype(bf16)
        ecb_ref[d["rows"], :] = d["ecum_x"][:, SSD_GW:]
        decb_ref[d["c"]] = d["dec"][:, SSD_GW:]
        return d

    st_ref[...] = jnp.zeros_like(st_ref)

    def pass_a(i, carry):
        ds = [stage1(i * SSD_UA + u) for u in range(SSD_UA)]
        ds = [stage2(d) for d in ds]
        ds = [stage3(d) for d in ds]
        state = st_ref[...]
        for d in ds:
            y_ref[d["rows"], :] = d["y"] + _dot(d["cm"], state.astype(bf16)) * d["ecum_x"][:, 0:SSD_GW]
            state = state * d["dec"][0:1, 0:SSD_GW] + d["cs_f"]
        st_ref[...] = state
        return carry

    lax.fori_loop(0, n_chunks // SSD_UA, pass_a, 0)

    st_ref[...] = jnp.zeros_like(st_ref)

    def pass_b(i, carry):
        chunks = [n_chunks - 1 - (i * SSD_UB + u) for u in range(SSD_UB)]
        rows = [pl.ds(pl.multiple_of(c * L, L), L) for c in chunks]
        cs = [_dot(bmt_ref[c], xinb_ref[r, :]) for c, r in zip(chunks, rows)]
        state = st_ref[...]
        ys = []
        for c, r, cs_b in zip(chunks, rows, cs):
            ys.append(y_ref[r, :] + _dot(c_ref[r, :], state.astype(bf16)) * ecb_ref[r, :])
            state = state * decb_ref[c][0:1, :] + cs_b
        st_ref[...] = state
        for r, y in zip(rows, ys):
            y = y * _silu(z_ref[r, :].astype(f32))
            ms = jnp.mean(y * y, axis=-1, keepdims=True)
            o_ref[r, :] = (y * lax.rsqrt(ms + EPS) * ng_ref[...]).astype(o_ref.dtype)
        return carry

    lax.fori_loop(0, n_chunks // SSD_UB, pass_b, 0)


def _ssd(pr, dt, dtb_row, alog_row, dskip_row, norm_row, batch, seq):
    m = pr.shape[0]
    expand = np.zeros((SSD_L, 2 * SSD_GW), np.float32)
    for e in range(SSD_E):
        expand[e, 64 * e:64 * (e + 1)] = 1.0
        expand[SSD_E + e, SSD_GW + 64 * e:SSD_GW + 64 * (e + 1)] = 1.0
    expand = jnp.asarray(expand, dtype=bf16)
    return pl.pallas_call(
        functools.partial(_ssd_body, seq=seq),
        grid=(batch, SSD_GROUPS),
        in_specs=[
            pl.BlockSpec((seq, SSD_GW), lambda b, g: (b, g)),
            pl.BlockSpec((seq, SSD_GW), lambda b, g: (b, 2 + g)),
            pl.BlockSpec((seq, SSD_STATE), lambda b, g: (b, 8 + g)),
            pl.BlockSpec((seq, SSD_STATE), lambda b, g: (b, 10 + g)),
            pl.BlockSpec((seq, 128), lambda b, g: (b, g)),
            pl.BlockSpec((None, 1, 128), lambda b, g: (g, 0, 0)),
            pl.BlockSpec((None, 1, 128), lambda b, g: (g, 0, 0)),
            pl.BlockSpec((None, 1, SSD_GW), lambda b, g: (g, 0, 0)),
            pl.BlockSpec((None, 1, SSD_GW), lambda b, g: (g, 0, 0)),
            pl.BlockSpec((SSD_L, 2 * SSD_GW), lambda b, g: (0, 0)),
        ],
        out_specs=pl.BlockSpec((seq, SSD_GW), lambda b, g: (b, g)),
        out_shape=jax.ShapeDtypeStruct((m, SSD_WIDTH), bf16),
        scratch_shapes=[
            pltpu.VMEM((seq // SSD_L, SSD_STATE, SSD_L), bf16),
            pltpu.VMEM((seq, SSD_GW), f32),
            pltpu.VMEM((seq, SSD_GW), f32),
            pltpu.VMEM((seq, SSD_GW), bf16),
            pltpu.VMEM((seq // SSD_L, 8, SSD_GW), f32),
            pltpu.VMEM((SSD_STATE, SSD_GW), f32),
        ],
        compiler_params=_cparams(2),
        name="ssd",
    )(pr, pr, pr, pr, dt, dtb_row, alog_row, dskip_row, norm_row, expand)


OUT_TM = 512
OUT_HALO = 16


def _outproj_body(att_ref, ssm_ref, gb_ref, gcp_ref, gc_ref, gcn_ref, hcp_ref, hc_ref, hcn_ref,
                  x_ref, w_ref, cw_ref, cb_ref, ng_ref, gmat_ref, o_ref, stage_ref, *, tiles_per_seq):
    i = pl.program_id(0)
    keep_p = jnp.where(i % tiles_per_seq > 0, 1.0, 0.0)
    keep_n = jnp.where(i % tiles_per_seq < tiles_per_seq - 1, 1.0, 0.0)
    tm = OUT_TM
    h = OUT_HALO
    slabs = []
    for sl in range(CONV_WIDTH // LANES):
        cs = slice(sl * LANES, (sl + 1) * LANES)
        cur = gc_ref[:, cs].astype(f32) * hc_ref[:, cs].astype(f32)
        stage_ref[sl, pl.ds(0, h, stride=2), :] = gcp_ref[:, cs].astype(f32) * hcp_ref[:, cs].astype(f32) * keep_p
        stage_ref[sl, pl.ds(2 * h, tm, stride=2), :] = cur
        stage_ref[sl, pl.ds(2 * (h + tm), h, stride=2), :] = (
            gcn_ref[:, cs].astype(f32) * hcn_ref[:, cs].astype(f32) * keep_n)
        slabs.append(cb_ref[:, cs] + cw_ref[0:1, cs] * stage_ref[sl, pl.ds(2 * (h - 1), tm, stride=2), :]
                     + cw_ref[1:2, cs] * cur
                     + cw_ref[2:3, cs] * stage_ref[sl, pl.ds(2 * (h + 1), tm, stride=2), :])
    y = gb_ref[...].astype(f32) * jnp.concatenate(slabs, axis=-1)
    ms = _dot_parts(_split2(y * y), gmat_ref[...]) * (1.0 / (CONV_WIDTH // CONV_GROUPS))
    sc = (y * lax.rsqrt(ms + EPS) * ng_ref[...]).astype(bf16)
    mix = jnp.concatenate([att_ref[...], ssm_ref[...], sc], axis=-1)
    o_ref[...] = x_ref[...] + _dot(mix, w_ref[...])


def _outproj(att, ssm, pr, x2, w_out, sc_w, sc_b, sc_g, seq):
    m = x2.shape[0]
    tm, h = OUT_TM, OUT_HALO
    per = tm // h
    nb = m // h
    const = lambda i: (0, 0)
    prev = lambda col: (lambda i: (jnp.maximum(i * per - 1, 0), col))
    nxt = lambda col: (lambda i: (jnp.minimum((i + 1) * per, nb - 1), col))
    return pl.pallas_call(
        functools.partial(_outproj_body, tiles_per_seq=seq // tm),
        grid=(m // tm,),
        in_specs=[
            pl.BlockSpec((tm, ATT_WIDTH), lambda i: (i, 0)),
            pl.BlockSpec((tm, SSD_WIDTH), lambda i: (i, 0)),
            pl.BlockSpec((tm, CONV_WIDTH), lambda i: (i, 3)),
            pl.BlockSpec((h, CONV_WIDTH), prev(4)),
            pl.BlockSpec((tm, CONV_WIDTH), lambda i: (i, 4)),
            pl.BlockSpec((h, CONV_WIDTH), nxt(4)),
            pl.BlockSpec((h, CONV_WIDTH), prev(5)),
            pl.BlockSpec((tm, CONV_WIDTH), lambda i: (i, 5)),
            pl.BlockSpec((h, CONV_WIDTH), nxt(5)),
            pl.BlockSpec((tm, D_MODEL), lambda i: (i, 0)),
            pl.BlockSpec((D_MIX, D_MODEL), const, pipeline_mode=pl.Buffered(1)),
            pl.BlockSpec((3, CONV_WIDTH), const),
            pl.BlockSpec((1, CONV_WIDTH), const),
            pl.BlockSpec((1, CONV_WIDTH), const),
            pl.BlockSpec((CONV_WIDTH, CONV_WIDTH), const),
        ],
        out_specs=pl.BlockSpec((tm, D_MODEL), lambda i: (i, 0)),
        out_shape=jax.ShapeDtypeStruct((m, D_MODEL), f32),
        scratch_shapes=[pltpu.VMEM((CONV_WIDTH // LANES, 2 * (tm + 2 * h), LANES), f32)],
        compiler_params=_cparams(1),
        name="outproj",
    )(att, ssm, pr, pr, pr, pr, pr, pr, pr, x2, w_out, sc_w, sc_b, sc_g,
      _group_ones(CONV_WIDTH, CONV_WIDTH // CONV_GROUPS))


FFN_TM = 512
FFN_HALO = 8
FFN_CHUNK = 256


def _ffn_body(xp_ref, x_ref, xn_ref, g_ref, wup_ref, cw_ref, cb_ref, wdn_ref, fg_ref, o_ref,
              hn_ref, u_ref, act_ref, *, tiles_per_seq, final_norm):
    i = pl.program_id(0)
    tm, h = FFN_TM, FFN_HALO
    keep_p = jnp.where(i % tiles_per_seq > 0, 1.0, 0.0)
    keep_n = jnp.where(i % tiles_per_seq < tiles_per_seq - 1, 1.0, 0.0)
    xall = jnp.concatenate([xp_ref[...] * keep_p, x_ref[...], xn_ref[...] * keep_n], axis=0)
    ms = jnp.mean(xall * xall, axis=-1, keepdims=True)
    hn_ref[...] = ((xall * lax.rsqrt(ms + EPS)) * g_ref[...]).astype(bf16)
    rows = tm + 2 * h
    for c0 in range(0, D_FF, FFN_CHUNK):
        halves = []
        for part in range(2):
            col = part * D_FF + c0
            u = _dot(hn_ref[...], wup_ref[:, col:col + FFN_CHUNK])
            slabs = []
            for sl in range(FFN_CHUNK // LANES):
                u_ref[part, sl, pl.ds(0, rows, stride=2), :] = u[:, sl * LANES:(sl + 1) * LANES]
                cs = slice(col + sl * LANES, col + (sl + 1) * LANES)
                slabs.append(cb_ref[:, cs]
                             + cw_ref[0:1, cs] * u_ref[part, sl, pl.ds(2 * (h - 1), tm, stride=2), :]
                             + cw_ref[1:2, cs] * u[h:h + tm, sl * LANES:(sl + 1) * LANES]
                             + cw_ref[2:3, cs] * u_ref[part, sl, pl.ds(2 * (h + 1), tm, stride=2), :])
            halves.append(jnp.concatenate(slabs, axis=-1))
        act_ref[:, c0:c0 + FFN_CHUNK] = (_silu(halves[0]) * halves[1]).astype(bf16)
    out = x_ref[...] + _dot(act_ref[...], wdn_ref[...])
    if final_norm:
        ms2 = jnp.mean(out * out, axis=-1, keepdims=True)
        out = (out * lax.rsqrt(ms2 + EPS)) * fg_ref[...]
    o_ref[...] = out


def _ffn(x2, g, w_up, conv_w, conv_b, w_down, final_g, seq, final_norm):
    m = x2.shape[0]
    tm, h = FFN_TM, FFN_HALO
    per = tm // h
    nb = m // h
    const = lambda i: (0, 0)
    return pl.pallas_call(
        functools.partial(_ffn_body, tiles_per_seq=seq // tm, final_norm=final_norm),
        grid=(m // tm,),
        in_specs=[
            pl.BlockSpec((h, D_MODEL), lambda i: (jnp.maximum(i * per - 1, 0), 0)),
            pl.BlockSpec((tm, D_MODEL), lambda i: (i, 0)),
            pl.BlockSpec((h, D_MODEL), lambda i: (jnp.minimum((i + 1) * per, nb - 1), 0)),
            pl.BlockSpec((1, D_MODEL), const),
            pl.BlockSpec((D_MODEL, 2 * D_FF), const, pipeline_mode=pl.Buffered(1)),
            pl.BlockSpec((3, 2 * D_FF), const),
            pl.BlockSpec((1, 2 * D_FF), const),
            pl.BlockSpec((D_FF, D_MODEL), const, pipeline_mode=pl.Buffered(1)),
            pl.BlockSpec((1, D_MODEL), const),
        ],
        out_specs=pl.BlockSpec((tm, D_MODEL), lambda i: (i, 0)),
        out_shape=jax.ShapeDtypeStruct((m, D_MODEL), f32),
        scratch_shapes=[
            pltpu.VMEM((tm + 2 * h, D_MODEL), bf16),
            pltpu.VMEM((2, FFN_CHUNK // LANES, 2 * (tm + 2 * h), LANES), f32),
            pltpu.VMEM((tm, D_FF), bf16),
        ],
        compiler_params=_cparams(1),
        name="convffn",
    )(x2, x2, x2, g, w_up, conv_w, conv_b, w_down, final_g)


def _layer_params(i, w_in, ssd_dt_bias, ssd_a_log, ssd_d, ssd_norm):
    w = w_in[i]
    dt0 = 3 * ATT_WIDTH + SSD_WIDTH + SSD_WIDTH + 2 * SSD_GROUPS * SSD_STATE
    w_main = jnp.concatenate([w[:, :dt0], w[:, dt0 + 2 * SSD_HEADS:]], axis=1).astype(bf16)
    w_dt_cols = w[:, dt0:dt0 + 2 * SSD_HEADS]

    def per_group_lanes(v2):
        v = v2.reshape(2, SSD_GROUPS, SSD_E).transpose(1, 0, 2).reshape(SSD_GROUPS, 1, 2 * SSD_E)
        return jnp.pad(v, ((0, 0), (0, 0), (0, 128 - 2 * SSD_E)))

    wd = w_dt_cols.reshape(D_MODEL, 2, SSD_GROUPS, SSD_E).transpose(0, 2, 1, 3)
    wd = wd.reshape(D_MODEL, SSD_GROUPS, 2 * SSD_E)
    w_dt = jnp.pad(wd, ((0, 0), (0, 0), (0, 128 - 2 * SSD_E))).reshape(D_MODEL, DT_W).astype(bf16)

    dtb_row = per_group_lanes(ssd_dt_bias[i])
    alog_row = per_group_lanes(ssd_a_log[i])
    dskip_row = jnp.repeat(ssd_d[i], HEAD_DIM).reshape(SSD_GROUPS, 1, SSD_GW)
    norm_row = ssd_norm[i].reshape(SSD_GROUPS, 1, SSD_GW)
    return w_main, w_dt, dtb_row, alog_row, dskip_row, norm_row


def kernel(x, mix_norm, w_in, ssd_conv_w, ssd_conv_b, ssd_dt_bias, ssd_a_log, ssd_d, ssd_norm, sc_conv_w, sc_conv_b, attn_norm, sc_norm, w_out, ffn_norm, w_up, ffn_conv_w, ffn_conv_b, w_down, final_norm):
    batch, seq, d = x.shape
    depth = w_in.shape[0]
    x2 = x.reshape(batch * seq, d)
    slopes = jnp.asarray(2.0 ** (-8.0 * (np.arange(ATT_HEADS) + 1.0) / ATT_HEADS), dtype=f32)
    for i in range(depth):
        w_main, w_dt, dtb_row, alog_row, dskip_row, norm_row = _layer_params(
            i, w_in, ssd_dt_bias, ssd_a_log, ssd_d, ssd_norm)
        qkv, pr, dt = _inproj(x2, mix_norm[i][None, :], w_main, w_dt, ssd_conv_w[i],
                              ssd_conv_b[i][None, :], seq)
        att = _attention(qkv, attn_norm[i][None, :], slopes, batch, seq)
        ssm = _ssd(pr, dt, dtb_row, alog_row, dskip_row, norm_row, batch, seq)
        x2 = _outproj(att, ssm, pr, x2, w_out[i].astype(bf16), sc_conv_w[i], sc_conv_b[i][None, :],
                      sc_norm[i][None, :], seq)
        x2 = _ffn(x2, ffn_norm[i][None, :], w_up[i].astype(bf16), ffn_conv_w[i],
                  ffn_conv_b[i][None, :], w_down[i].astype(bf16), final_norm[None, :], seq,
                  final_norm=(i == depth - 1))
    return x2.reshape(batch, seq, d)
```

```python
import functools

import numpy as np
import jax
import jax.numpy as jnp
from jax import lax
from jax.experimental import pallas as pl
from jax.experimental.pallas import tpu as pltpu

f32 = jnp.float32
bf16 = jnp.bfloat16

EPS = 1e-6
NEG = -1e30
LOG2E = 1.4426950408889634

D_MODEL = 1024
HEAD_DIM = 64
ATT_HEADS = 8
ATT_WIDTH = 512
DILATED_BRANCHES = ((128, 1), (512, 4), (2048, 16))
ATT_HALF = 64
SSD_HEADS = 8
SSD_WIDTH = 512
SSD_GROUPS = 2
SSD_STATE = 128
SSD_CONV = 5
SSD_CHUNK = 128
CONV_WIDTH = 512
CONV_GROUPS = 8
D_MIX = 1536
D_FF = 2816
D_IN = 4624

VMEM_LIMIT = 56 * 1024 * 1024
LANES = 128

_NT = (((1,), (1,)), ((), ()))


def _cparams(n_axes):
    return pltpu.CompilerParams(
        dimension_semantics=("arbitrary",) * n_axes, vmem_limit_bytes=VMEM_LIMIT)


def _split2(x):
    hi = x.astype(bf16)
    lo = (x - hi.astype(f32)).astype(bf16)
    return hi, lo


def _split3(x):
    h1 = x.astype(bf16)
    r1 = x - h1.astype(f32)
    h2 = r1.astype(bf16)
    h3 = (r1 - h2.astype(f32)).astype(bf16)
    return h1, h2, h3


def _dot(a, b):
    return jnp.dot(a, b, preferred_element_type=f32)


def _dot_parts(parts, mat):
    acc = _dot(parts[0], mat)
    for p in parts[1:]:
        acc = acc + _dot(p, mat)
    return acc


def _parts_dot(mat, parts):
    acc = _dot(mat, parts[0])
    for p in parts[1:]:
        acc = acc + _dot(mat, p)
    return acc


def _silu(x):
    return x * jax.nn.sigmoid(x)


def _group_ones(width, group):
    idx = np.arange(width) // group
    return jnp.asarray(idx[:, None] == idx[None, :], dtype=bf16)


IN_TM = 512
IN_CHUNK = 512
QKV_W = 3 * ATT_WIDTH
PR_W = 3072
DT_W = 256
XBC_LO, XBC_HI = 512, 1536
IN_HALO = 8


def _inproj_body(xp_ref, x_ref, xn_ref, g_ref, w_ref, wdt_ref, cw_ref, cb_ref,
                 qkv_ref, pr_ref, dt_ref, hn_ref, stage_ref, *, tiles_per_seq):
    i = pl.program_id(0)
    tm, h = IN_TM, IN_HALO
    keep_p = jnp.where(i % tiles_per_seq > 0, 1.0, 0.0)
    keep_n = jnp.where(i % tiles_per_seq < tiles_per_seq - 1, 1.0, 0.0)
    x = x_ref[...]
    ms = jnp.mean(x * x, axis=-1, keepdims=True)
    hn = ((x * lax.rsqrt(ms + EPS)) * g_ref[...]).astype(bf16)
    halo = jnp.concatenate([xp_ref[...] * keep_p, xn_ref[...] * keep_n], axis=0)
    msh = jnp.mean(halo * halo, axis=-1, keepdims=True)
    hh = ((halo * lax.rsqrt(msh + EPS)) * g_ref[...]).astype(bf16)
    hn_ref[0:2 * h, :] = hh
    hn_ref[2 * h:, :] = hn
    half = SSD_CONV // 2
    for slot, c0 in enumerate(range(XBC_LO, XBC_HI, IN_CHUNK)):
        u = _dot(hn_ref[...], w_ref[:, QKV_W + c0:QKV_W + c0 + IN_CHUNK])
        slabs = []
        for sl in range(IN_CHUNK // LANES):
            ls = slice(sl * LANES, (sl + 1) * LANES)
            cs = slice(c0 - XBC_LO + sl * LANES, c0 - XBC_LO + (sl + 1) * LANES)
            cur = u[2 * h:, ls]
            stage_ref[slot, sl, pl.ds(0, h, stride=2), :] = u[0:h, ls]
            stage_ref[slot, sl, pl.ds(2 * h, tm, stride=2), :] = cur
            stage_ref[slot, sl, pl.ds(2 * (h + tm), h, stride=2), :] = u[h:2 * h, ls]
            acc = cb_ref[:, cs] + cw_ref[half:half + 1, cs] * cur
            for kk in range(SSD_CONV):
                if kk != half:
                    acc = acc + cw_ref[kk:kk + 1, cs] * stage_ref[
                        slot, sl, pl.ds(2 * (h - half + kk), tm, stride=2), :]
            slabs.append(acc)
        pr_ref[:, c0:c0 + IN_CHUNK] = _silu(jnp.concatenate(slabs, axis=-1)).astype(bf16)
    for c0 in range(0, QKV_W, IN_CHUNK):
        qkv_ref[:, c0:c0 + IN_CHUNK] = _dot(hn, w_ref[:, c0:c0 + IN_CHUNK])
    for c0 in list(range(0, XBC_LO, IN_CHUNK)) + list(range(XBC_HI, PR_W, IN_CHUNK)):
        pr_ref[:, c0:c0 + IN_CHUNK] = _dot(
            hn, w_ref[:, QKV_W + c0:QKV_W + c0 + IN_CHUNK]).astype(bf16)
    dt_ref[...] = _dot(hn, wdt_ref[...])


def _inproj(x2, g, w_main, w_dt, conv_w, conv_b, seq):
    m = x2.shape[0]
    tm, h = IN_TM, IN_HALO
    per = tm // h
    nb = m // h
    const = lambda i: (0, 0)
    return pl.pallas_call(
        functools.partial(_inproj_body, tiles_per_seq=seq // tm),
        grid=(m // IN_TM,),
        in_specs=[
            pl.BlockSpec((h, D_MODEL), lambda i: (jnp.maximum(i * per - 1, 0), 0)),
            pl.BlockSpec((IN_TM, D_MODEL), lambda i: (i, 0)),
            pl.BlockSpec((h, D_MODEL), lambda i: (jnp.minimum((i + 1) * per, nb - 1), 0)),
            pl.BlockSpec((1, D_MODEL), const),
            pl.BlockSpec((D_MODEL, QKV_W + PR_W), const, pipeline_mode=pl.Buffered(1)),
            pl.BlockSpec((D_MODEL, DT_W), const, pipeline_mode=pl.Buffered(1)),
            pl.BlockSpec((SSD_CONV, XBC_HI - XBC_LO), const),
            pl.BlockSpec((1, XBC_HI - XBC_LO), const),
        ],
        out_specs=[
            pl.BlockSpec((IN_TM, QKV_W), lambda i: (i, 0)),
            pl.BlockSpec((IN_TM, PR_W), lambda i: (i, 0)),
            pl.BlockSpec((IN_TM, DT_W), lambda i: (i, 0)),
        ],
        out_shape=[
            jax.ShapeDtypeStruct((m, QKV_W), f32),
            jax.ShapeDtypeStruct((m, PR_W), bf16),
            jax.ShapeDtypeStruct((m, DT_W), f32),
        ],
        scratch_shapes=[
            pltpu.VMEM((tm + 2 * h, D_MODEL), bf16),
            pltpu.VMEM(((XBC_HI - XBC_LO) // IN_CHUNK, IN_CHUNK // LANES, 2 * (tm + 2 * h), LANES), f32),
        ],
        compiler_params=_cparams(1),
        name="inproj",
    )(x2, x2, x2, g, w_main, w_dt, conv_w, conv_b)


ATT_TQ = 128
ATT_TK = ATT_TQ + 2 * ATT_HALF
ATT_FIN = 512
ATT_UNROLL = 16
ATT_OFFSETS = (0, -ATT_HALF, -2 * ATT_HALF)
ATT_PRE = 4


def _attn_body(slopes_ref, q_ref, k_ref, v_ref, g_ref, gmat_ref, o_ref,
               acc_ref, m_ref, l_ref, bias_ref, d4_ref, unperm_ref, *, seq):
    pair = pl.program_id(1)
    lane = lax.broadcasted_iota(jnp.int32, (1, 2 * HEAD_DIM), 1)
    head0 = lane < HEAD_DIM
    qscale = HEAD_DIM ** -0.5 * LOG2E
    qmask0 = jnp.where(head0, qscale, 0.0)
    qmask1 = jnp.where(head0, 0.0, qscale)

    row = lax.broadcasted_iota(jnp.int32, (2 * ATT_TQ, ATT_TK), 0)
    col = lax.broadcasted_iota(jnp.int32, (2 * ATT_TQ, ATT_TK), 1)
    rel0 = col - (row & (ATT_TQ - 1))
    slope_rows = jnp.where(row < ATT_TQ, slopes_ref[2 * pair], slopes_ref[2 * pair + 1])
    for bi, (_, r) in enumerate(DILATED_BRANCHES):
        for vi, off in enumerate(ATT_OFFSETS):
            dist = jnp.abs(rel0 + off)
            bias_ref[bi, vi] = jnp.where(
                dist <= ATT_HALF, dist.astype(f32) * (slope_rows * (-float(r) * LOG2E)), NEG)

    pre_len = seq // ATT_PRE

    def deinterleave(j, carry):
        for a, ref in enumerate((q_ref, k_ref, v_ref)):
            for c in range(ATT_PRE):
                d4_ref[a, pl.ds(c * pre_len + pl.multiple_of(j * ATT_FIN, ATT_FIN), ATT_FIN), :] = (
                    ref[pl.ds(c + ATT_PRE * pl.multiple_of(j * ATT_FIN, ATT_FIN), ATT_FIN, stride=ATT_PRE), :])
        return carry

    lax.fori_loop(0, pre_len // ATT_FIN, deinterleave, 0)

    for bi, (_, r) in enumerate(DILATED_BRANCHES):
        sub_len = seq // r
        tiles_per_class = sub_len // ATT_TQ

        def scores(t, r=r, sub_len=sub_len, tiles_per_class=tiles_per_class):
            cls = t // tiles_per_class
            t0 = (t % tiles_per_class) * ATT_TQ
            k0 = jnp.clip(t0 - ATT_HALF, 0, sub_len - ATT_TK)
            variant = jnp.where(t0 == 0, 0, jnp.where(t0 == sub_len - ATT_TQ, 2, 1))
            if r > ATT_PRE:
                base = (cls % ATT_PRE) * (seq // ATT_PRE) + cls // ATT_PRE
                q_rows = pl.ds(base + (r // ATT_PRE) * t0, ATT_TQ, stride=r // ATT_PRE)
                q = d4_ref[0, q_rows, :]
                k_rows = pl.ds(base + (r // ATT_PRE) * k0, ATT_TK, stride=r // ATT_PRE)
                k = d4_ref[1, k_rows, :]
            else:
                q_rows = pl.ds(cls + r * t0, ATT_TQ, stride=r)
                q = q_ref[q_rows, :]
                k_rows = pl.ds(cls + r * k0, ATT_TK, stride=r)
                k = k_ref[k_rows, :]
            q2 = jnp.concatenate([q * qmask0, q * qmask1], axis=0).astype(bf16)
            s = lax.dot_general(q2, k.astype(bf16), _NT, preferred_element_type=f32)
            return q_rows, k_rows, variant, s

        def softmax(unit, bi=bi):
            q_rows, k_rows, variant, s = unit
            s = s + bias_ref[bi, variant]
            m = jnp.max(s, axis=-1, keepdims=True)
            return q_rows, k_rows, jnp.exp2(s - m).astype(bf16), m

        def values(unit, bi=bi, r=r):
            q_rows, k_rows, p, m = unit
            v = d4_ref[2, k_rows, :] if r > ATT_PRE else v_ref[k_rows, :]
            o0 = _dot(p[0:ATT_TQ], jnp.where(head0, v, 1.0).astype(bf16))
            o1 = _dot(p[ATT_TQ:], jnp.where(head0, 1.0, v).astype(bf16))
            acc_ref[bi, q_rows, :] = jnp.where(head0, o0, o1)
            l_ref[bi, q_rows, :] = jnp.where(head0, o1, o0)
            m_ref[bi, q_rows, :] = jnp.where(head0, m[0:ATT_TQ], m[ATT_TQ:])

        def tile_group(g, carry):
            st_s, st_p = {}, {}
            for step in range(ATT_UNROLL + 2):
                if step < ATT_UNROLL:
                    st_s[step] = scores(g * ATT_UNROLL + step)
                if 0 <= step - 1 < ATT_UNROLL:
                    st_p[step - 1] = softmax(st_s.pop(step - 1))
                if 0 <= step - 2 < ATT_UNROLL:
                    values(st_p.pop(step - 2))
            return carry

        lax.fori_loop(0, seq // (ATT_TQ * ATT_UNROLL), tile_group, 0)

    gmat = gmat_ref[...]

    def combine(t, carry):
        r0 = pl.multiple_of(t * ATT_FIN, ATT_FIN)
        rows = pl.ds(r0, ATT_FIN)
        part = ATT_FIN // ATT_PRE

        def branch_rows(ref, slot, b):
            if DILATED_BRANCHES[b][1] <= ATT_PRE:
                return ref[b, rows, :]
            for c in range(ATT_PRE):
                unperm_ref[slot, pl.ds(c, part, stride=ATT_PRE), :] = (
                    ref[b, pl.ds(c * pre_len + pl.multiple_of(t * part, part), part), :])
            return unperm_ref[slot]

        ms_ = [branch_rows(m_ref, 0, b) for b in range(len(DILATED_BRANCHES))]
        mx = functools.reduce(jnp.maximum, ms_)
        num, den = None, None
        for b, mb in enumerate(ms_):
            e = jnp.exp2(mb - mx)
            nb = e * branch_rows(acc_ref, 1, b)
            db = e * pltpu.roll(branch_rows(l_ref, 2, b), HEAD_DIM, axis=1)
            num = nb if num is None else num + nb
            den = db if den is None else den + db
        y = num / den
        ms = _dot_parts(_split2(y * y), gmat) * (1.0 / HEAD_DIM)
        o_ref[rows, :] = (y * lax.rsqrt(ms + EPS) * g_ref[...]).astype(o_ref.dtype)
        return carry

    lax.fori_loop(0, seq // ATT_FIN, combine, 0)


def _attention(qkv, attn_norm_row, slopes, batch, seq):
    m = qkv.shape[0]
    pairs = ATT_HEADS // 2
    w = 2 * HEAD_DIM
    nbr = len(DILATED_BRANCHES)
    return pl.pallas_call(
        functools.partial(_attn_body, seq=seq),
        grid=(batch, pairs),
        in_specs=[
            pl.BlockSpec(memory_space=pltpu.SMEM),
            pl.BlockSpec((seq, w), lambda b, p: (b, p)),
            pl.BlockSpec((seq, w), lambda b, p: (b, pairs + p)),
            pl.BlockSpec((seq, w), lambda b, p: (b, 2 * pairs + p)),
            pl.BlockSpec((1, w), lambda b, p: (0, p)),
            pl.BlockSpec((w, w), lambda b, p: (0, 0)),
        ],
        out_specs=pl.BlockSpec((seq, w), lambda b, p: (b, p)),
        out_shape=jax.ShapeDtypeStruct((m, ATT_WIDTH), bf16),
        scratch_shapes=[
            pltpu.VMEM((nbr, seq, w), f32),
            pltpu.VMEM((nbr, seq, w), f32),
            pltpu.VMEM((nbr, seq, w), f32),
            pltpu.VMEM((nbr, len(ATT_OFFSETS), 2 * ATT_TQ, ATT_TK), f32),
            pltpu.VMEM((3, seq, w), f32),
            pltpu.VMEM((3, ATT_FIN, w), f32),
        ],
        compiler_params=_cparams(2),
        name="dilated_attention",
    )(slopes, qkv, qkv, qkv, attn_norm_row, _group_ones(w, HEAD_DIM))


SSD_L = SSD_CHUNK
SSD_GW = SSD_WIDTH // SSD_GROUPS
SSD_E = SSD_HEADS // SSD_GROUPS
SSD_UA = 4
SSD_UB = 4


def _ssd_body(z_ref, xs_ref, b_ref, c_ref, dt_ref, dtb_ref, alog_ref,
              dsk_ref, ng_ref, expand_ref, o_ref,
              bmt_ref, y_ref, ecb_ref, xinb_ref, decb_ref, st_ref, *, seq):
    n_chunks = seq // SSD_L
    L = SSD_L
    row_i = lax.broadcasted_iota(jnp.int32, (L, L), 0)
    col_i = lax.broadcasted_iota(jnp.int32, (L, L), 1)
    lower = row_i >= col_i
    upper = row_i <= col_i
    lower_b = lower.astype(bf16)
    upper_b = upper.astype(bf16)
    lane8 = lax.broadcasted_iota(jnp.int32, (1, L), 1)
    fwd_lane = lane8 < SSD_E
    a_row = -jnp.exp(alog_ref[...])
    expand = expand_ref[...]
    lane_blk = lax.broadcasted_iota(jnp.int32, (1, SSD_GW), 1) // 64


    def stage1(c):
        rows = pl.ds(pl.multiple_of(c * L, L), L)
        d = dict(c=c, rows=rows, xs=xs_ref[rows, :].astype(f32), bm=b_ref[rows, :], cm=c_ref[rows, :])
        dtv = jax.nn.softplus(dt_ref[rows, :] + dtb_ref[...])
        adt = dtv * a_row
        parts = _split2(adt)
        d["cum"] = jnp.where(fwd_lane, _parts_dot(lower_b, parts), _parts_dot(upper_b, parts))
        d["tot"] = jnp.sum(adt, axis=0, keepdims=True)
        d["dtv"] = dtv
        return d

    def stage2(d):
        cum, tot, dtv = d["cum"], d["tot"], d["dtv"]
        d["src_t"] = (cum - jnp.log(dtv)).T
        d["gram"] = lax.dot_general(d["cm"], d["bm"], _NT, preferred_element_type=f32)
        d["bm_t"] = d["bm"].T
        bmt_ref[d["c"]] = d["bm_t"]
        dec_hi, dec_lo = _split2(jnp.broadcast_to(jnp.exp(tot), (8, L)))
        wide = _dot(jnp.concatenate(
            [jnp.exp(cum).astype(bf16), (jnp.exp(tot - cum) * dtv).astype(bf16), dec_hi, dec_lo], axis=0), expand)
        d["ecum_x"], d["w_x"] = wide[0:L], wide[L:2 * L]
        d["dec"] = wide[2 * L:2 * L + 8] + wide[2 * L + 8:]
        return d

    def stage3(d):
        xs, cum, src_t, gram = d["xs"], d["cum"], d["src_t"], d["gram"]
        m_rows = []
        for e in range(SSD_E):
            df = jnp.exp(jnp.where(lower, cum[:, e:e + 1] - src_t[e:e + 1, :], NEG))
            db = jnp.exp(jnp.where(upper, cum[:, SSD_E + e:SSD_E + e + 1]
                                   - src_t[SSD_E + e:SSD_E + e + 1, :], NEG))
            m_rows.append((gram * (df + db)).astype(bf16))
        y_all = _dot(jnp.concatenate(m_rows, axis=0), xs_ref[d["rows"], :])
        y = dsk_ref[...] * xs
        for e in range(SSD_E):
            y = jnp.where(lane_blk == e, y_all[e * L:(e + 1) * L, :] + y, y)
        d["y"] = y
        d["cs_f"] = _dot(d["bm_t"], (xs * d["w_x"][:, 0:SSD_GW]).astype(bf16))
        xinb_ref[d["rows"], :] = (xs * d["w_x"][:, SSD_GW:]).astype(bf16)
        ecb_ref[d["rows"], :] = d["ecum_x"][:, SSD_GW:]
        decb_ref[d["c"]] = d["dec"][:, SSD_GW:]
        return d

    st_ref[...] = jnp.zeros_like(st_ref)

    def pass_a(i, carry):
        ds = [stage1(i * SSD_UA + u) for u in range(SSD_UA)]
        ds = [stage2(d) for d in ds]
        ds = [stage3(d) for d in ds]
        state = st_ref[...]
        for d in ds:
            y_ref[d["rows"], :] = d["y"] + _dot(d["cm"], state.astype(bf16)) * d["ecum_x"][:, 0:SSD_GW]
            state = state * d["dec"][0:1, 0:SSD_GW] + d["cs_f"]
        st_ref[...] = state
        return carry

    lax.fori_loop(0, n_chunks // SSD_UA, pass_a, 0)

    st_ref[...] = jnp.zeros_like(st_ref)

    def pass_b(i, carry):
        chunks = [n_chunks - 1 - (i * SSD_UB + u) for u in range(SSD_UB)]
        rows = [pl.ds(pl.multiple_of(c * L, L), L) for c in chunks]
        cs = [_dot(bmt_ref[c], xinb_ref[r, :]) for c, r in zip(chunks, rows)]
        state = st_ref[...]
        ys = []
        for c, r, cs_b in zip(chunks, rows, cs):
            ys.append(y_ref[r, :] + _dot(c_ref[r, :], state.astype(bf16)) * ecb_ref[r, :])
            state = state * decb_ref[c][0:1, :] + cs_b
        st_ref[...] = state
        for r, y in zip(rows, ys):
            y = y * _silu(z_ref[r, :].astype(f32))
            ms = jnp.mean(y * y, axis=-1, keepdims=True)
            o_ref[r, :] = (y * lax.rsqrt(ms + EPS) * ng_ref[...]).astype(o_ref.dtype)
        return carry

    lax.fori_loop(0, n_chunks // SSD_UB, pass_b, 0)


def _ssd(pr, dt, dtb_row, alog_row, dskip_row, norm_row, batch, seq):
    m = pr.shape[0]
    expand = np.zeros((SSD_L, 2 * SSD_GW), np.float32)
    for e in range(SSD_E):
        expand[e, 64 * e:64 * (e + 1)] = 1.0
        expand[SSD_E + e, SSD_GW + 64 * e:SSD_GW + 64 * (e + 1)] = 1.0
    expand = jnp.asarray(expand, dtype=bf16)
    return pl.pallas_call(
        functools.partial(_ssd_body, seq=seq),
        grid=(batch, SSD_GROUPS),
        in_specs=[
            pl.BlockSpec((seq, SSD_GW), lambda b, g: (b, g)),
            pl.BlockSpec((seq, SSD_GW), lambda b, g: (b, 2 + g)),
            pl.BlockSpec((seq, SSD_STATE), lambda b, g: (b, 8 + g)),
            pl.BlockSpec((seq, SSD_STATE), lambda b, g: (b, 10 + g)),
            pl.BlockSpec((seq, 128), lambda b, g: (b, g)),
            pl.BlockSpec((None, 1, 128), lambda b, g: (g, 0, 0)),
            pl.BlockSpec((None, 1, 128), lambda b, g: (g, 0, 0)),
            pl.BlockSpec((None, 1, SSD_GW), lambda b, g: (g, 0, 0)),
            pl.BlockSpec((None, 1, SSD_GW), lambda b, g: (g, 0, 0)),
            pl.BlockSpec((SSD_L, 2 * SSD_GW), lambda b, g: (0, 0)),
        ],
        out_specs=pl.BlockSpec((seq, SSD_GW), lambda b, g: (b, g)),
        out_shape=jax.ShapeDtypeStruct((m, SSD_WIDTH), bf16),
        scratch_shapes=[
            pltpu.VMEM((seq // SSD_L, SSD_STATE, SSD_L), bf16),
            pltpu.VMEM((seq, SSD_GW), f32),
            pltpu.VMEM((seq, SSD_GW), f32),
            pltpu.VMEM((seq, SSD_GW), bf16),
            pltpu.VMEM((seq // SSD_L, 8, SSD_GW), f32),
            pltpu.VMEM((SSD_STATE, SSD_GW), f32),
        ],
        compiler_params=_cparams(2),
        name="ssd",
    )(pr, pr, pr, pr, dt, dtb_row, alog_row, dskip_row, norm_row, expand)


OUT_TM = 512
OUT_HALO = 16


def _outproj_body(att_ref, ssm_ref, gb_ref, gcp_ref, gc_ref, gcn_ref, hcp_ref, hc_ref, hcn_ref,
                  x_ref, w_ref, cw_ref, cb_ref, ng_ref, gmat_ref, o_ref, stage_ref, *, tiles_per_seq):
    i = pl.program_id(0)
    keep_p = jnp.where(i % tiles_per_seq > 0, 1.0, 0.0)
    keep_n = jnp.where(i % tiles_per_seq < tiles_per_seq - 1, 1.0, 0.0)
    tm = OUT_TM
    h = OUT_HALO
    slabs = []
    for sl in range(CONV_WIDTH // LANES):
        cs = slice(sl * LANES, (sl + 1) * LANES)
        cur = gc_ref[:, cs].astype(f32) * hc_ref[:, cs].astype(f32)
        stage_ref[sl, pl.ds(0, h, stride=2), :] = gcp_ref[:, cs].astype(f32) * hcp_ref[:, cs].astype(f32) * keep_p
        stage_ref[sl, pl.ds(2 * h, tm, stride=2), :] = cur
        stage_ref[sl, pl.ds(2 * (h + tm), h, stride=2), :] = (
            gcn_ref[:, cs].astype(f32) * hcn_ref[:, cs].astype(f32) * keep_n)
        slabs.append(cb_ref[:, cs] + cw_ref[0:1, cs] * stage_ref[sl, pl.ds(2 * (h - 1), tm, stride=2), :]
                     + cw_ref[1:2, cs] * cur
                     + cw_ref[2:3, cs] * stage_ref[sl, pl.ds(2 * (h + 1), tm, stride=2), :])
    y = gb_ref[...].astype(f32) * jnp.concatenate(slabs, axis=-1)
    ms = _dot_parts(_split2(y * y), gmat_ref[...]) * (1.0 / (CONV_WIDTH // CONV_GROUPS))
    sc = (y * lax.rsqrt(ms + EPS) * ng_ref[...]).astype(bf16)
    mix = jnp.concatenate([att_ref[...], ssm_ref[...], sc], axis=-1)
    o_ref[...] = x_ref[...] + _dot(mix, w_ref[...])


def _outproj(att, ssm, pr, x2, w_out, sc_w, sc_b, sc_g, seq):
    m = x2.shape[0]
    tm, h = OUT_TM, OUT_HALO
    per = tm // h
    nb = m // h
    const = lambda i: (0, 0)
    prev = lambda col: (lambda i: (jnp.maximum(i * per - 1, 0), col))
    nxt = lambda col: (lambda i: (jnp.minimum((i + 1) * per, nb - 1), col))
    return pl.pallas_call(
        functools.partial(_outproj_body, tiles_per_seq=seq // tm),
        grid=(m // tm,),
        in_specs=[
            pl.BlockSpec((tm, ATT_WIDTH), lambda i: (i, 0)),
            pl.BlockSpec((tm, SSD_WIDTH), lambda i: (i, 0)),
            pl.BlockSpec((tm, CONV_WIDTH), lambda i: (i, 3)),
            pl.BlockSpec((h, CONV_WIDTH), prev(4)),
            pl.BlockSpec((tm, CONV_WIDTH), lambda i: (i, 4)),
            pl.BlockSpec((h, CONV_WIDTH), nxt(4)),
            pl.BlockSpec((h, CONV_WIDTH), prev(5)),
            pl.BlockSpec((tm, CONV_WIDTH), lambda i: (i, 5)),
            pl.BlockSpec((h, CONV_WIDTH), nxt(5)),
            pl.BlockSpec((tm, D_MODEL), lambda i: (i, 0)),
            pl.BlockSpec((D_MIX, D_MODEL), const, pipeline_mode=pl.Buffered(1)),
            pl.BlockSpec((3, CONV_WIDTH), const),
            pl.BlockSpec((1, CONV_WIDTH), const),
            pl.BlockSpec((1, CONV_WIDTH), const),
            pl.BlockSpec((CONV_WIDTH, CONV_WIDTH), const),
        ],
        out_specs=pl.BlockSpec((tm, D_MODEL), lambda i: (i, 0)),
        out_shape=jax.ShapeDtypeStruct((m, D_MODEL), f32),
        scratch_shapes=[pltpu.VMEM((CONV_WIDTH // LANES, 2 * (tm + 2 * h), LANES), f32)],
        compiler_params=_cparams(1),
        name="outproj",
    )(att, ssm, pr, pr, pr, pr, pr, pr, pr, x2, w_out, sc_w, sc_b, sc_g,
      _group_ones(CONV_WIDTH, CONV_WIDTH // CONV_GROUPS))


FFN_TM = 512
FFN_HALO = 8
FFN_CHUNK = 256


def _ffn_body(xp_ref, x_ref, xn_ref, g_ref, wup_ref, cw_ref, cb_ref, wdn_ref, fg_ref, o_ref,
              hn_ref, u_ref, act_ref, *, tiles_per_seq, final_norm):
    i = pl.program_id(0)
    tm, h = FFN_TM, FFN_HALO
    keep_p = jnp.where(i % tiles_per_seq > 0, 1.0, 0.0)
    keep_n = jnp.where(i % tiles_per_seq < tiles_per_seq - 1, 1.0, 0.0)
    xall = jnp.concatenate([xp_ref[...] * keep_p, x_ref[...], xn_ref[...] * keep_n], axis=0)
    ms = jnp.mean(xall * xall, axis=-1, keepdims=True)
    hn_ref[...] = ((xall * lax.rsqrt(ms + EPS)) * g_ref[...]).astype(bf16)
    rows = tm + 2 * h
    for c0 in range(0, D_FF, FFN_CHUNK):
        halves = []
        for part in range(2):
            col = part * D_FF + c0
            u = _dot(hn_ref[...], wup_ref[:, col:col + FFN_CHUNK])
            slabs = []
            for sl in range(FFN_CHUNK // LANES):
                u_ref[part, sl, pl.ds(0, rows, stride=2), :] = u[:, sl * LANES:(sl + 1) * LANES]
                cs = slice(col + sl * LANES, col + (sl + 1) * LANES)
                slabs.append(cb_ref[:, cs]
                             + cw_ref[0:1, cs] * u_ref[part, sl, pl.ds(2 * (h - 1), tm, stride=2), :]
                             + cw_ref[1:2, cs] * u[h:h + tm, sl * LANES:(sl + 1) * LANES]
                             + cw_ref[2:3, cs] * u_ref[part, sl, pl.ds(2 * (h + 1), tm, stride=2), :])
            halves.append(jnp.concatenate(slabs, axis=-1))
        act_ref[:, c0:c0 + FFN_CHUNK] = (_silu(halves[0]) * halves[1]).astype(bf16)
    out = x_ref[...] + _dot(act_ref[...], wdn_ref[...])
    if final_norm:
        ms2 = jnp.mean(out * out, axis=-1, keepdims=True)
        out = (out * lax.rsqrt(ms2 + EPS)) * fg_ref[...]
    o_ref[...] = out


def _ffn(x2, g, w_up, conv_w, conv_b, w_down, final_g, seq, final_norm):
    m = x2.shape[0]
    tm, h = FFN_TM, FFN_HALO
    per = tm // h
    nb = m // h
    const = lambda i: (0, 0)
    return pl.pallas_call(
        functools.partial(_ffn_body, tiles_per_seq=seq // tm, final_norm=final_norm),
        grid=(m // tm,),
        in_specs=[
            pl.BlockSpec((h, D_MODEL), lambda i: (jnp.maximum(i * per - 1, 0), 0)),
            pl.BlockSpec((tm, D_MODEL), lambda i: (i, 0)),
            pl.BlockSpec((h, D_MODEL), lambda i: (jnp.minimum((i + 1) * per, nb - 1), 0)),
            pl.BlockSpec((1, D_MODEL), const),
            pl.BlockSpec((D_MODEL, 2 * D_FF), const, pipeline_mode=pl.Buffered(1)),
            pl.BlockSpec((3, 2 * D_FF), const),
            pl.BlockSpec((1, 2 * D_FF), const),
            pl.BlockSpec((D_FF, D_MODEL), const, pipeline_mode=pl.Buffered(1)),
            pl.BlockSpec((1, D_MODEL), const),
        ],
        out_specs=pl.BlockSpec((tm, D_MODEL), lambda i: (i, 0)),
        out_shape=jax.ShapeDtypeStruct((m, D_MODEL), f32),
        scratch_shapes=[
            pltpu.VMEM((tm + 2 * h, D_MODEL), bf16),
            pltpu.VMEM((2, FFN_CHUNK // LANES, 2 * (tm + 2 * h), LANES), f32),
            pltpu.VMEM((tm, D_FF), bf16),
        ],
        compiler_params=_cparams(1),
        name="convffn",
    )(x2, x2, x2, g, w_up, conv_w, conv_b, w_down, final_g)


def _layer_params(i, w_in, ssd_dt_bias, ssd_a_log, ssd_d, ssd_norm):
    w = w_in[i]
    dt0 = 3 * ATT_WIDTH + SSD_WIDTH + SSD_WIDTH + 2 * SSD_GROUPS * SSD_STATE
    w_main = jnp.concatenate([w[:, :dt0], w[:, dt0 + 2 * SSD_HEADS:]], axis=1).astype(bf16)
    w_dt_cols = w[:, dt0:dt0 + 2 * SSD_HEADS]

    def per_group_lanes(v2):
        v = v2.reshape(2, SSD_GROUPS, SSD_E).transpose(1, 0, 2).reshape(SSD_GROUPS, 1, 2 * SSD_E)
        return jnp.pad(v, ((0, 0), (0, 0), (0, 128 - 2 * SSD_E)))

    wd = w_dt_cols.reshape(D_MODEL, 2, SSD_GROUPS, SSD_E).transpose(0, 2, 1, 3)
    wd = wd.reshape(D_MODEL, SSD_GROUPS, 2 * SSD_E)
    w_dt = jnp.pad(wd, ((0, 0), (0, 0), (0, 128 - 2 * SSD_E))).reshape(D_MODEL, DT_W).astype(bf16)

    dtb_row = per_group_lanes(ssd_dt_bias[i])
    alog_row = per_group_lanes(ssd_a_log[i])
    dskip_row = jnp.repeat(ssd_d[i], HEAD_DIM).reshape(SSD_GROUPS, 1, SSD_GW)
    norm_row = ssd_norm[i].reshape(SSD_GROUPS, 1, SSD_GW)
    return w_main, w_dt, dtb_row, alog_row, dskip_row, norm_row


def kernel(x, mix_norm, w_in, ssd_conv_w, ssd_conv_b, ssd_dt_bias, ssd_a_log, ssd_d, ssd_norm, sc_conv_w, sc_conv_b, attn_norm, sc_norm, w_out, ffn_norm, w_up, ffn_conv_w, ffn_conv_b, w_down, final_norm):
    batch, seq, d = x.shape
    depth = w_in.shape[0]
    x2 = x.reshape(batch * seq, d)
    slopes = jnp.asarray(2.0 ** (-8.0 * (np.arange(ATT_HEADS) + 1.0) / ATT_HEADS), dtype=f32)
    for i in range(depth):
        w_main, w_dt, dtb_row, alog_row, dskip_row, norm_row = _layer_params(
            i, w_in, ssd_dt_bias, ssd_a_log, ssd_d, ssd_norm)
        qkv, pr, dt = _inproj(x2, mix_norm[i][None, :], w_main, w_dt, ssd_conv_w[i],
                              ssd_conv_b[i][None, :], seq)
        att = _attention(qkv, attn_norm[i][None, :], slopes, batch, seq)
        ssm = _ssd(pr, dt, dtb_row, alog_row, dskip_row, norm_row, batch, seq)
        x2 = _outproj(att, ssm, pr, x2, w_out[i].astype(bf16), sc_conv_w[i], sc_conv_b[i][None, :],
                      sc_norm[i][None, :], seq)
        x2 = _ffn(x2, ffn_norm[i][None, :], w_up[i].astype(bf16), ffn_conv_w[i],
                  ffn_conv_b[i][None, :], w_down[i].astype(bf16), final_norm[None, :], seq,
                  final_norm=(i == depth - 1))
    return x2.reshape(batch, seq, d)
```

```python
import functools

import numpy as np
import jax
import jax.numpy as jnp
from jax import lax
from jax.experimental import pallas as pl
from jax.experimental.pallas import tpu as pltpu

f32 = jnp.float32
bf16 = jnp.bfloat16

EPS = 1e-6
NEG = -1e30
LOG2E = 1.4426950408889634

D_MODEL = 1024
HEAD_DIM = 64
ATT_HEADS = 8
ATT_WIDTH = 512
DILATED_BRANCHES = ((128, 1), (512, 4), (2048, 16))
ATT_HALF = 64
SSD_HEADS = 8
SSD_WIDTH = 512
SSD_GROUPS = 2
SSD_STATE = 128
SSD_CONV = 5
SSD_CHUNK = 128
CONV_WIDTH = 512
CONV_GROUPS = 8
D_MIX = 1536
D_FF = 2816
D_IN = 4624

VMEM_LIMIT = 56 * 1024 * 1024
LANES = 128

_NT = (((1,), (1,)), ((), ()))


def _layer_spec(layer, *shape, **kw):
    return pl.BlockSpec((None,) + shape, lambda *_: (layer,) + (0,) * len(shape), **kw)


def _cparams(n_axes):
    return pltpu.CompilerParams(
        dimension_semantics=("arbitrary",) * n_axes, vmem_limit_bytes=VMEM_LIMIT)


def _split2(x):
    hi = x.astype(bf16)
    lo = (x - hi.astype(f32)).astype(bf16)
    return hi, lo


def _dot(a, b):
    return jnp.dot(a, b, preferred_element_type=f32)


def _dot_parts(parts, mat):
    acc = _dot(parts[0], mat)
    for p in parts[1:]:
        acc = acc + _dot(p, mat)
    return acc


def _parts_dot(mat, parts):
    acc = _dot(mat, parts[0])
    for p in parts[1:]:
        acc = acc + _dot(mat, p)
    return acc


def _silu(x):
    return x * jax.nn.sigmoid(x)


def _group_ones(width, group):
    idx = np.arange(width) // group
    return jnp.asarray(idx[:, None] == idx[None, :], dtype=bf16)


IN_TM = 512
IN_CHUNK = 512
QKV_W = 3 * ATT_WIDTH
PR_W = 3072
DT_W = 256
XBC_LO, XBC_HI = 512, 1536
IN_HALO = 8


def _inproj_body(xp_ref, x_ref, xn_ref, g_ref, w_ref, wdt_ref, cw_ref, cb_ref,
                 qkv_ref, pr_ref, dt_ref, hn_ref, stage_ref, *, tiles_per_seq):
    i = pl.program_id(0)
    tm, h = IN_TM, IN_HALO
    keep_p = jnp.where(i % tiles_per_seq > 0, 1.0, 0.0)
    keep_n = jnp.where(i % tiles_per_seq < tiles_per_seq - 1, 1.0, 0.0)
    x = x_ref[...]
    ms = jnp.mean(x * x, axis=-1, keepdims=True)
    hn = ((x * lax.rsqrt(ms + EPS)) * g_ref[...]).astype(bf16)
    halo = jnp.concatenate([xp_ref[...] * keep_p, xn_ref[...] * keep_n], axis=0)
    msh = jnp.mean(halo * halo, axis=-1, keepdims=True)
    hh = ((halo * lax.rsqrt(msh + EPS)) * g_ref[...]).astype(bf16)
    hn_ref[0:2 * h, :] = hh
    hn_ref[2 * h:, :] = hn
    half = SSD_CONV // 2
    for slot, c0 in enumerate(range(XBC_LO, XBC_HI, IN_CHUNK)):
        u = _dot(hn_ref[...], w_ref[:, QKV_W + c0:QKV_W + c0 + IN_CHUNK])
        slabs = []
        for sl in range(IN_CHUNK // LANES):
            ls = slice(sl * LANES, (sl + 1) * LANES)
            cs = slice(c0 - XBC_LO + sl * LANES, c0 - XBC_LO + (sl + 1) * LANES)
            cur = u[2 * h:, ls]
            stage_ref[slot, sl, pl.ds(0, h, stride=2), :] = u[0:h, ls]
            stage_ref[slot, sl, pl.ds(2 * h, tm, stride=2), :] = cur
            stage_ref[slot, sl, pl.ds(2 * (h + tm), h, stride=2), :] = u[h:2 * h, ls]
            acc = cb_ref[:, cs] + cw_ref[half:half + 1, cs] * cur
            for kk in range(SSD_CONV):
                if kk != half:
                    acc = acc + cw_ref[kk:kk + 1, cs] * stage_ref[
                        slot, sl, pl.ds(2 * (h - half + kk), tm, stride=2), :]
            slabs.append(acc)
        pr_ref[:, c0:c0 + IN_CHUNK] = _silu(jnp.concatenate(slabs, axis=-1)).astype(bf16)
    for c0 in range(0, QKV_W, IN_CHUNK):
        qkv_ref[:, c0:c0 + IN_CHUNK] = _dot(hn, w_ref[:, c0:c0 + IN_CHUNK])
    for c0 in list(range(0, XBC_LO, IN_CHUNK)) + list(range(XBC_HI, PR_W, IN_CHUNK)):
        pr_ref[:, c0:c0 + IN_CHUNK] = _dot(
            hn, w_ref[:, QKV_W + c0:QKV_W + c0 + IN_CHUNK]).astype(bf16)
    dt_ref[...] = _dot(hn, wdt_ref[...])


def _inproj(layer, x2, g, w_main, w_dt, conv_w, conv_b, seq):
    m = x2.shape[0]
    tm, h = IN_TM, IN_HALO
    per = tm // h
    nb = m // h
    const = lambda i: (0, 0)
    return pl.pallas_call(
        functools.partial(_inproj_body, tiles_per_seq=seq // tm),
        grid=(m // IN_TM,),
        in_specs=[
            pl.BlockSpec((h, D_MODEL), lambda i: (jnp.maximum(i * per - 1, 0), 0)),
            pl.BlockSpec((IN_TM, D_MODEL), lambda i: (i, 0)),
            pl.BlockSpec((h, D_MODEL), lambda i: (jnp.minimum((i + 1) * per, nb - 1), 0)),
            _layer_spec(layer, 1, D_MODEL),
            _layer_spec(layer, D_MODEL, QKV_W + PR_W, pipeline_mode=pl.Buffered(1)),
            _layer_spec(layer, D_MODEL, DT_W, pipeline_mode=pl.Buffered(1)),
            _layer_spec(layer, SSD_CONV, XBC_HI - XBC_LO),
            _layer_spec(layer, 1, XBC_HI - XBC_LO),
        ],
        out_specs=[
            pl.BlockSpec((IN_TM, QKV_W), lambda i: (i, 0)),
            pl.BlockSpec((IN_TM, PR_W), lambda i: (i, 0)),
            pl.BlockSpec((IN_TM, DT_W), lambda i: (i, 0)),
        ],
        out_shape=[
            jax.ShapeDtypeStruct((m, QKV_W), f32),
            jax.ShapeDtypeStruct((m, PR_W), bf16),
            jax.ShapeDtypeStruct((m, DT_W), f32),
        ],
        scratch_shapes=[
            pltpu.VMEM((tm + 2 * h, D_MODEL), bf16),
            pltpu.VMEM(((XBC_HI - XBC_LO) // IN_CHUNK, IN_CHUNK // LANES, 2 * (tm + 2 * h), LANES), f32),
        ],
        compiler_params=_cparams(1),
        name="inproj",
    )(x2, x2, x2, g, w_main, w_dt, conv_w, conv_b)


ATT_TQ = 128
ATT_TK = ATT_TQ + 2 * ATT_HALF
ATT_FIN = 512
ATT_UNROLL = 16
ATT_OFFSETS = (0, -ATT_HALF, -2 * ATT_HALF)
ATT_PRE = 4


def _attn_body(slopes_ref, q_ref, k_ref, v_ref, g_ref, gmat_ref, o_ref,
               acc_ref, m_ref, l_ref, bias_ref, d4_ref, unperm_ref, *, seq):
    pair = pl.program_id(1)
    lane = lax.broadcasted_iota(jnp.int32, (1, 2 * HEAD_DIM), 1)
    head0 = lane < HEAD_DIM
    qscale = HEAD_DIM ** -0.5 * LOG2E
    qmask0 = jnp.where(head0, qscale, 0.0)
    qmask1 = jnp.where(head0, 0.0, qscale)

    row = lax.broadcasted_iota(jnp.int32, (2 * ATT_TQ, ATT_TK), 0)
    col = lax.broadcasted_iota(jnp.int32, (2 * ATT_TQ, ATT_TK), 1)
    rel0 = col - (row & (ATT_TQ - 1))
    slope_rows = jnp.where(row < ATT_TQ, slopes_ref[2 * pair], slopes_ref[2 * pair + 1])
    for bi, (_, r) in enumerate(DILATED_BRANCHES):
        for vi, off in enumerate(ATT_OFFSETS):
            dist = jnp.abs(rel0 + off)
            bias_ref[bi, vi] = jnp.where(
                dist <= ATT_HALF, dist.astype(f32) * (slope_rows * (-float(r) * LOG2E)), NEG)

    pre_len = seq // ATT_PRE

    def deinterleave(j, carry):
        for a, ref in enumerate((q_ref, k_ref, v_ref)):
            for c in range(ATT_PRE):
                d4_ref[a, pl.ds(c * pre_len + pl.multiple_of(j * ATT_FIN, ATT_FIN), ATT_FIN), :] = (
                    ref[pl.ds(c + ATT_PRE * pl.multiple_of(j * ATT_FIN, ATT_FIN), ATT_FIN, stride=ATT_PRE), :])
        return carry

    lax.fori_loop(0, pre_len // ATT_FIN, deinterleave, 0)

    for bi, (_, r) in enumerate(DILATED_BRANCHES):
        sub_len = seq // r
        tiles_per_class = sub_len // ATT_TQ

        def scores(t, r=r, sub_len=sub_len, tiles_per_class=tiles_per_class):
            cls = t // tiles_per_class
            t0 = (t % tiles_per_class) * ATT_TQ
            k0 = jnp.clip(t0 - ATT_HALF, 0, sub_len - ATT_TK)
            variant = jnp.where(t0 == 0, 0, jnp.where(t0 == sub_len - ATT_TQ, 2, 1))
            if r > ATT_PRE:
                base = (cls % ATT_PRE) * (seq // ATT_PRE) + cls // ATT_PRE
                q_rows = pl.ds(base + (r // ATT_PRE) * t0, ATT_TQ, stride=r // ATT_PRE)
                q = d4_ref[0, q_rows, :]
                k_rows = pl.ds(base + (r // ATT_PRE) * k0, ATT_TK, stride=r // ATT_PRE)
                k = d4_ref[1, k_rows, :]
            else:
                q_rows = pl.ds(cls + r * t0, ATT_TQ, stride=r)
                q = q_ref[q_rows, :]
                k_rows = pl.ds(cls + r * k0, ATT_TK, stride=r)
                k = k_ref[k_rows, :]
            q2 = jnp.concatenate([q * qmask0, q * qmask1], axis=0).astype(bf16)
            s = lax.dot_general(q2, k.astype(bf16), _NT, preferred_element_type=f32)
            return q_rows, k_rows, variant, s

        def softmax(unit, bi=bi):
            q_rows, k_rows, variant, s = unit
            s = s + bias_ref[bi, variant]
            m = jnp.max(s, axis=-1, keepdims=True)
            return q_rows, k_rows, jnp.exp2(s - m).astype(bf16), m

        def values(unit, bi=bi, r=r):
            q_rows, k_rows, p, m = unit
            v = d4_ref[2, k_rows, :] if r > ATT_PRE else v_ref[k_rows, :]
            o0 = _dot(p[0:ATT_TQ], jnp.where(head0, v, 1.0).astype(bf16))
            o1 = _dot(p[ATT_TQ:], jnp.where(head0, 1.0, v).astype(bf16))
            acc_ref[bi, q_rows, :] = jnp.where(head0, o0, o1)
            l_ref[bi, q_rows, :] = jnp.where(head0, o1, o0)
            m_ref[bi, q_rows, :] = jnp.where(head0, m[0:ATT_TQ], m[ATT_TQ:])

        def tile_group(g, carry):
            st_s, st_p = {}, {}
            for step in range(ATT_UNROLL + 2):
                if step < ATT_UNROLL:
                    st_s[step] = scores(g * ATT_UNROLL + step)
                if 0 <= step - 1 < ATT_UNROLL:
                    st_p[step - 1] = softmax(st_s.pop(step - 1))
                if 0 <= step - 2 < ATT_UNROLL:
                    values(st_p.pop(step - 2))
            return carry

        lax.fori_loop(0, seq // (ATT_TQ * ATT_UNROLL), tile_group, 0)

    gmat = gmat_ref[...]

    def combine(t, carry):
        r0 = pl.multiple_of(t * ATT_FIN, ATT_FIN)
        rows = pl.ds(r0, ATT_FIN)
        part = ATT_FIN // ATT_PRE

        def branch_rows(ref, slot, b):
            if DILATED_BRANCHES[b][1] <= ATT_PRE:
                return ref[b, rows, :]
            for c in range(ATT_PRE):
                unperm_ref[slot, pl.ds(c, part, stride=ATT_PRE), :] = (
                    ref[b, pl.ds(c * pre_len + pl.multiple_of(t * part, part), part), :])
            return unperm_ref[slot]

        ms_ = [branch_rows(m_ref, 0, b) for b in range(len(DILATED_BRANCHES))]
        mx = functools.reduce(jnp.maximum, ms_)
        num, den = None, None
        for b, mb in enumerate(ms_):
            e = jnp.exp2(mb - mx)
            nb = e * branch_rows(acc_ref, 1, b)
            db = e * pltpu.roll(branch_rows(l_ref, 2, b), HEAD_DIM, axis=1)
            num = nb if num is None else num + nb
            den = db if den is None else den + db
        y = num / den
        ms = _dot_parts(_split2(y * y), gmat) * (1.0 / HEAD_DIM)
        o_ref[rows, :] = (y * lax.rsqrt(ms + EPS) * g_ref[...]).astype(o_ref.dtype)
        return carry

    lax.fori_loop(0, seq // ATT_FIN, combine, 0)


def _attention(layer, qkv, attn_norm_row, slopes, batch, seq):
    m = qkv.shape[0]
    pairs = ATT_HEADS // 2
    w = 2 * HEAD_DIM
    nbr = len(DILATED_BRANCHES)
    return pl.pallas_call(
        functools.partial(_attn_body, seq=seq),
        grid=(batch, pairs),
        in_specs=[
            pl.BlockSpec(memory_space=pltpu.SMEM),
            pl.BlockSpec((seq, w), lambda b, p: (b, p)),
            pl.BlockSpec((seq, w), lambda b, p: (b, pairs + p)),
            pl.BlockSpec((seq, w), lambda b, p: (b, 2 * pairs + p)),
            pl.BlockSpec((None, 1, w), lambda b, p: (layer, 0, p)),
            pl.BlockSpec((w, w), lambda b, p: (0, 0)),
        ],
        out_specs=pl.BlockSpec((seq, w), lambda b, p: (b, p)),
        out_shape=jax.ShapeDtypeStruct((m, ATT_WIDTH), bf16),
        scratch_shapes=[
            pltpu.VMEM((nbr, seq, w), f32),
            pltpu.VMEM((nbr, seq, w), f32),
            pltpu.VMEM((nbr, seq, w), f32),
            pltpu.VMEM((nbr, len(ATT_OFFSETS), 2 * ATT_TQ, ATT_TK), f32),
            pltpu.VMEM((3, seq, w), f32),
            pltpu.VMEM((3, ATT_FIN, w), f32),
        ],
        compiler_params=_cparams(2),
        name="dilated_attention",
    )(slopes, qkv, qkv, qkv, attn_norm_row, _group_ones(w, HEAD_DIM))


SSD_L = SSD_CHUNK
SSD_GW = SSD_WIDTH // SSD_GROUPS
SSD_E = SSD_HEADS // SSD_GROUPS
SSD_UA = 4
SSD_UB = 8


def _ssd_body(z_ref, xs_ref, b_ref, c_ref, dt_ref, dtb_ref, alog_ref,
              dsk_ref, ng_ref, expand_ref, o_ref,
              bmt_ref, y_ref, ecb_ref, xinb_ref, decb_ref, st_ref, *, seq):
    n_chunks = seq // SSD_L
    L = SSD_L
    row_i = lax.broadcasted_iota(jnp.int32, (L, L), 0)
    col_i = lax.broadcasted_iota(jnp.int32, (L, L), 1)
    lower = row_i >= col_i
    upper = row_i <= col_i
    lower_b = lower.astype(bf16)
    upper_b = upper.astype(bf16)
    lane8 = lax.broadcasted_iota(jnp.int32, (1, L), 1)
    fwd_lane = lane8 < SSD_E
    a_row = -jnp.exp(alog_ref[...])
    expand = expand_ref[...]
    lane_blk = lax.broadcasted_iota(jnp.int32, (1, SSD_GW), 1) // 64


    def stage1(c):
        rows = pl.ds(pl.multiple_of(c * L, L), L)
        d = dict(c=c, rows=rows, xs=xs_ref[rows, :].astype(f32), bm=b_ref[rows, :], cm=c_ref[rows, :])
        dtv = jax.nn.softplus(dt_ref[rows, :] + dtb_ref[...])
        adt = dtv * a_row
        parts = _split2(adt)
        d["cum"] = jnp.where(fwd_lane, _parts_dot(lower_b, parts), _parts_dot(upper_b, parts))
        d["tot"] = jnp.sum(adt, axis=0, keepdims=True)
        d["dtv"] = dtv
        return d

    def stage2(d):
        cum, tot, dtv = d["cum"], d["tot"], d["dtv"]
        d["src_t"] = (cum - jnp.log(dtv)).T
        d["gram"] = lax.dot_general(d["cm"], d["bm"], _NT, preferred_element_type=f32)
        d["bm_t"] = d["bm"].T
        bmt_ref[d["c"]] = d["bm_t"]
        dec_hi, dec_lo = _split2(jnp.broadcast_to(jnp.exp(tot), (8, L)))
        wide = _dot(jnp.concatenate(
            [jnp.exp(cum).astype(bf16), (jnp.exp(tot - cum) * dtv).astype(bf16), dec_hi, dec_lo], axis=0), expand)
        d["ecum_x"], d["w_x"] = wide[0:L], wide[L:2 * L]
        d["dec"] = wide[2 * L:2 * L + 8] + wide[2 * L + 8:]
        return d

    def stage3(d):
        xs, cum, src_t, gram = d["xs"], d["cum"], d["src_t"], d["gram"]
        m_rows = []
        for e in range(SSD_E):
            df = jnp.exp(jnp.where(lower, cum[:, e:e + 1] - src_t[e:e + 1, :], NEG))
            db = jnp.exp(jnp.where(upper, cum[:, SSD_E + e:SSD_E + e + 1]
                                   - src_t[SSD_E + e:SSD_E + e + 1, :], NEG))
            m_rows.append((gram * (df + db)).astype(bf16))
        y_all = _dot(jnp.concatenate(m_rows, axis=0), xs_ref[d["rows"], :])
        y = dsk_ref[...] * xs
        for e in range(SSD_E):
            y = jnp.where(lane_blk == e, y_all[e * L:(e + 1) * L, :] + y, y)
        d["y"] = y
        d["cs_f"] = _dot(d["bm_t"], (xs * d["w_x"][:, 0:SSD_GW]).astype(bf16))
        xinb_ref[d["rows"], :] = (xs * d["w_x"][:, SSD_GW:]).astype(bf16)
        ecb_ref[d["rows"], :] = d["ecum_x"][:, SSD_GW:]
        decb_ref[d["c"]] = d["dec"][:, SSD_GW:]
        return d

    st_ref[...] = jnp.zeros_like(st_ref)

    def pass_a(i, carry):
        ds = [stage1(i * SSD_UA + u) for u in range(SSD_UA)]
        ds = [stage2(d) for d in ds]
        ds = [stage3(d) for d in ds]
        state = st_ref[...]
        for d in ds:
            y_ref[d["rows"], :] = d["y"] + _dot(d["cm"], state.astype(bf16)) * d["ecum_x"][:, 0:SSD_GW]
            state = state * d["dec"][0:1, 0:SSD_GW] + d["cs_f"]
        st_ref[...] = state
        return carry

    lax.fori_loop(0, n_chunks // SSD_UA, pass_a, 0)

    st_ref[...] = jnp.zeros_like(st_ref)

    def pass_b(i, carry):
        chunks = [n_chunks - 1 - (i * SSD_UB + u) for u in range(SSD_UB)]
        rows = [pl.ds(pl.multiple_of(c * L, L), L) for c in chunks]
        cs = [_dot(bmt_ref[c], xinb_ref[r, :]) for c, r in zip(chunks, rows)]
        state = st_ref[...]
        ys = []
        for c, r, cs_b in zip(chunks, rows, cs):
            ys.append(y_ref[r, :] + _dot(c_ref[r, :], state.astype(bf16)) * ecb_ref[r, :])
            state = state * decb_ref[c][0:1, :] + cs_b
        st_ref[...] = state
        for r, y in zip(rows, ys):
            y = y * _silu(z_ref[r, :].astype(f32))
            ms = jnp.mean(y * y, axis=-1, keepdims=True)
            o_ref[r, :] = (y * lax.rsqrt(ms + EPS) * ng_ref[...]).astype(o_ref.dtype)
        return carry

    lax.fori_loop(0, n_chunks // SSD_UB, pass_b, 0)


def _ssd(layer, pr, dt, dtb_row, alog_row, dskip_row, norm_row, batch, seq):
    m = pr.shape[0]
    expand = np.zeros((SSD_L, 2 * SSD_GW), np.float32)
    for e in range(SSD_E):
        expand[e, 64 * e:64 * (e + 1)] = 1.0
        expand[SSD_E + e, SSD_GW + 64 * e:SSD_GW + 64 * (e + 1)] = 1.0
    expand = jnp.asarray(expand, dtype=bf16)
    return pl.pallas_call(
        functools.partial(_ssd_body, seq=seq),
        grid=(batch, SSD_GROUPS),
        in_specs=[
            pl.BlockSpec((seq, SSD_GW), lambda b, g: (b, g)),
            pl.BlockSpec((seq, SSD_GW), lambda b, g: (b, 2 + g)),
            pl.BlockSpec((seq, SSD_STATE), lambda b, g: (b, 8 + g)),
            pl.BlockSpec((seq, SSD_STATE), lambda b, g: (b, 10 + g)),
            pl.BlockSpec((seq, 128), lambda b, g: (b, g)),
            pl.BlockSpec((None, None, 1, 128), lambda b, g: (layer, g, 0, 0)),
            pl.BlockSpec((None, None, 1, 128), lambda b, g: (layer, g, 0, 0)),
            pl.BlockSpec((None, None, 1, SSD_GW), lambda b, g: (layer, g, 0, 0)),
            pl.BlockSpec((None, None, 1, SSD_GW), lambda b, g: (layer, g, 0, 0)),
            pl.BlockSpec((SSD_L, 2 * SSD_GW), lambda b, g: (0, 0)),
        ],
        out_specs=pl.BlockSpec((seq, SSD_GW), lambda b, g: (b, g)),
        out_shape=jax.ShapeDtypeStruct((m, SSD_WIDTH), bf16),
        scratch_shapes=[
            pltpu.VMEM((seq // SSD_L, SSD_STATE, SSD_L), bf16),
            pltpu.VMEM((seq, SSD_GW), f32),
            pltpu.VMEM((seq, SSD_GW), f32),
            pltpu.VMEM((seq, SSD_GW), bf16),
            pltpu.VMEM((seq // SSD_L, 8, SSD_GW), f32),
            pltpu.VMEM((SSD_STATE, SSD_GW), f32),
        ],
        compiler_params=_cparams(2),
        name="ssd",
    )(pr, pr, pr, pr, dt, dtb_row, alog_row, dskip_row, norm_row, expand)


OUT_TM = 512
OUT_HALO = 16


def _outproj_body(att_ref, ssm_ref, gb_ref, gcp_ref, gc_ref, gcn_ref, hcp_ref, hc_ref, hcn_ref,
                  x_ref, w_ref, cw_ref, cb_ref, ng_ref, gmat_ref, o_ref, stage_ref, *, tiles_per_seq):
    i = pl.program_id(0)
    keep_p = jnp.where(i % tiles_per_seq > 0, 1.0, 0.0)
    keep_n = jnp.where(i % tiles_per_seq < tiles_per_seq - 1, 1.0, 0.0)
    tm = OUT_TM
    h = OUT_HALO
    ready = (_dot(att_ref[...], w_ref[0:ATT_WIDTH, :])
             + _dot(ssm_ref[...], w_ref[ATT_WIDTH:ATT_WIDTH + SSD_WIDTH, :]))
    slabs = []
    for sl in range(CONV_WIDTH // LANES):
        cs = slice(sl * LANES, (sl + 1) * LANES)
        cur = gc_ref[:, cs].astype(f32) * hc_ref[:, cs].astype(f32)
        stage_ref[sl, pl.ds(0, h, stride=2), :] = gcp_ref[:, cs].astype(f32) * hcp_ref[:, cs].astype(f32) * keep_p
        stage_ref[sl, pl.ds(2 * h, tm, stride=2), :] = cur
        stage_ref[sl, pl.ds(2 * (h + tm), h, stride=2), :] = (
            gcn_ref[:, cs].astype(f32) * hcn_ref[:, cs].astype(f32) * keep_n)
        slabs.append(cb_ref[:, cs] + cw_ref[0:1, cs] * stage_ref[sl, pl.ds(2 * (h - 1), tm, stride=2), :]
                     + cw_ref[1:2, cs] * cur
                     + cw_ref[2:3, cs] * stage_ref[sl, pl.ds(2 * (h + 1), tm, stride=2), :])
    y = gb_ref[...].astype(f32) * jnp.concatenate(slabs, axis=-1)
    ms = _dot_parts(_split2(y * y), gmat_ref[...]) * (1.0 / (CONV_WIDTH // CONV_GROUPS))
    sc = (y * lax.rsqrt(ms + EPS) * ng_ref[...]).astype(bf16)
    o_ref[...] = x_ref[...] + ready + _dot(sc, w_ref[ATT_WIDTH + SSD_WIDTH:, :])


def _outproj(layer, att, ssm, pr, x2, w_out, sc_w, sc_b, sc_g, seq):
    m = x2.shape[0]
    tm, h = OUT_TM, OUT_HALO
    per = tm // h
    nb = m // h
    const = lambda i: (0, 0)
    prev = lambda col: (lambda i: (jnp.maximum(i * per - 1, 0), col))
    nxt = lambda col: (lambda i: (jnp.minimum((i + 1) * per, nb - 1), col))
    return pl.pallas_call(
        functools.partial(_outproj_body, tiles_per_seq=seq // tm),
        grid=(m // tm,),
        in_specs=[
            pl.BlockSpec((tm, ATT_WIDTH), lambda i: (i, 0)),
            pl.BlockSpec((tm, SSD_WIDTH), lambda i: (i, 0)),
            pl.BlockSpec((tm, CONV_WIDTH), lambda i: (i, 3)),
            pl.BlockSpec((h, CONV_WIDTH), prev(4)),
            pl.BlockSpec((tm, CONV_WIDTH), lambda i: (i, 4)),
            pl.BlockSpec((h, CONV_WIDTH), nxt(4)),
            pl.BlockSpec((h, CONV_WIDTH), prev(5)),
            pl.BlockSpec((tm, CONV_WIDTH), lambda i: (i, 5)),
            pl.BlockSpec((h, CONV_WIDTH), nxt(5)),
            pl.BlockSpec((tm, D_MODEL), lambda i: (i, 0)),
            _layer_spec(layer, D_MIX, D_MODEL, pipeline_mode=pl.Buffered(1)),
            _layer_spec(layer, 3, CONV_WIDTH),
            _layer_spec(layer, 1, CONV_WIDTH),
            _layer_spec(layer, 1, CONV_WIDTH),
            pl.BlockSpec((CONV_WIDTH, CONV_WIDTH), const),
        ],
        out_specs=pl.BlockSpec((tm, D_MODEL), lambda i: (i, 0)),
        out_shape=jax.ShapeDtypeStruct((m, D_MODEL), f32),
        scratch_shapes=[pltpu.VMEM((CONV_WIDTH // LANES, 2 * (tm + 2 * h), LANES), f32)],
        compiler_params=_cparams(1),
        name="outproj",
    )(att, ssm, pr, pr, pr, pr, pr, pr, pr, x2, w_out, sc_w, sc_b, sc_g,
      _group_ones(CONV_WIDTH, CONV_WIDTH // CONV_GROUPS))


FFN_TM = 512
FFN_HALO = 8
FFN_CHUNK = 256


def _ffn_body(xp_ref, x_ref, xn_ref, g_ref, wup_ref, cw_ref, cb_ref, wdn_ref, fg_ref, o_ref,
              hn_ref, u_ref, act_ref, *, tiles_per_seq, final_norm):
    i = pl.program_id(0)
    tm, h = FFN_TM, FFN_HALO
    keep_p = jnp.where(i % tiles_per_seq > 0, 1.0, 0.0)
    keep_n = jnp.where(i % tiles_per_seq < tiles_per_seq - 1, 1.0, 0.0)
    xall = jnp.concatenate([xp_ref[...] * keep_p, x_ref[...], xn_ref[...] * keep_n], axis=0)
    ms = jnp.mean(xall * xall, axis=-1, keepdims=True)
    hn_ref[...] = ((xall * lax.rsqrt(ms + EPS)) * g_ref[...]).astype(bf16)
    rows = tm + 2 * h
    for c0 in range(0, D_FF, FFN_CHUNK):
        halves = []
        for part in range(2):
            col = part * D_FF + c0
            u = _dot(hn_ref[...], wup_ref[:, col:col + FFN_CHUNK])
            slabs = []
            for sl in range(FFN_CHUNK // LANES):
                u_ref[part, sl, pl.ds(0, rows, stride=2), :] = u[:, sl * LANES:(sl + 1) * LANES]
                cs = slice(col + sl * LANES, col + (sl + 1) * LANES)
                slabs.append(cb_ref[:, cs]
                             + cw_ref[0:1, cs] * u_ref[part, sl, pl.ds(2 * (h - 1), tm, stride=2), :]
                             + cw_ref[1:2, cs] * u[h:h + tm, sl * LANES:(sl + 1) * LANES]
                             + cw_ref[2:3, cs] * u_ref[part, sl, pl.ds(2 * (h + 1), tm, stride=2), :])
            halves.append(jnp.concatenate(slabs, axis=-1))
        act_ref[:, c0:c0 + FFN_CHUNK] = (_silu(halves[0]) * halves[1]).astype(bf16)
    out = x_ref[...] + _dot(act_ref[...], wdn_ref[...])
    if final_norm:
        ms2 = jnp.mean(out * out, axis=-1, keepdims=True)
        out = (out * lax.rsqrt(ms2 + EPS)) * fg_ref[...]
    o_ref[...] = out


def _ffn(layer, x2, g, w_up, conv_w, conv_b, w_down, final_g, seq, final_norm):
    m = x2.shape[0]
    tm, h = FFN_TM, FFN_HALO
    per = tm // h
    nb = m // h
    const = lambda i: (0, 0)
    return pl.pallas_call(
        functools.partial(_ffn_body, tiles_per_seq=seq // tm, final_norm=final_norm),
        grid=(m // tm,),
        in_specs=[
            pl.BlockSpec((h, D_MODEL), lambda i: (jnp.maximum(i * per - 1, 0), 0)),
            pl.BlockSpec((tm, D_MODEL), lambda i: (i, 0)),
            pl.BlockSpec((h, D_MODEL), lambda i: (jnp.minimum((i + 1) * per, nb - 1), 0)),
            _layer_spec(layer, 1, D_MODEL),
            _layer_spec(layer, D_MODEL, 2 * D_FF, pipeline_mode=pl.Buffered(1)),
            _layer_spec(layer, 3, 2 * D_FF),
            _layer_spec(layer, 1, 2 * D_FF),
            _layer_spec(layer, D_FF, D_MODEL, pipeline_mode=pl.Buffered(1)),
            pl.BlockSpec((1, D_MODEL), const),
        ],
        out_specs=pl.BlockSpec((tm, D_MODEL), lambda i: (i, 0)),
        out_shape=jax.ShapeDtypeStruct((m, D_MODEL), f32),
        scratch_shapes=[
            pltpu.VMEM((tm + 2 * h, D_MODEL), bf16),
            pltpu.VMEM((2, FFN_CHUNK // LANES, 2 * (tm + 2 * h), LANES), f32),
            pltpu.VMEM((tm, D_FF), bf16),
        ],
        compiler_params=_cparams(1),
        name="convffn",
    )(x2, x2, x2, g, w_up, conv_w, conv_b, w_down, final_g)


def _prepare(w_in, ssd_dt_bias, ssd_a_log, ssd_d, ssd_norm):
    depth = w_in.shape[0]
    dt0 = 3 * ATT_WIDTH + SSD_WIDTH + SSD_WIDTH + 2 * SSD_GROUPS * SSD_STATE
    w_main = jnp.concatenate([w_in[:, :, :dt0], w_in[:, :, dt0 + 2 * SSD_HEADS:]], axis=2).astype(bf16)
    w_dt_cols = w_in[:, :, dt0:dt0 + 2 * SSD_HEADS]

    def per_group_lanes(v):
        v = v.reshape(depth, 2, SSD_GROUPS, SSD_E).transpose(0, 2, 1, 3).reshape(depth, SSD_GROUPS, 1, 2 * SSD_E)
        return jnp.pad(v, ((0, 0), (0, 0), (0, 0), (0, 128 - 2 * SSD_E)))

    wd = w_dt_cols.reshape(depth, D_MODEL, 2, SSD_GROUPS, SSD_E).transpose(0, 1, 3, 2, 4)
    wd = wd.reshape(depth, D_MODEL, SSD_GROUPS, 2 * SSD_E)
    w_dt = jnp.pad(wd, ((0, 0), (0, 0), (0, 0), (0, 128 - 2 * SSD_E))).reshape(depth, D_MODEL, DT_W).astype(bf16)
    dtb_row = per_group_lanes(ssd_dt_bias)
    alog_row = per_group_lanes(ssd_a_log)
    dskip_row = jnp.repeat(ssd_d, HEAD_DIM, axis=-1).reshape(depth, SSD_GROUPS, 1, SSD_GW)
    norm_row = ssd_norm.reshape(depth, SSD_GROUPS, 1, SSD_GW)
    return w_main, w_dt, dtb_row, alog_row, dskip_row, norm_row


def kernel(x, mix_norm, w_in, ssd_conv_w, ssd_conv_b, ssd_dt_bias, ssd_a_log, ssd_d, ssd_norm, sc_conv_w, sc_conv_b, attn_norm, sc_norm, w_out, ffn_norm, w_up, ffn_conv_w, ffn_conv_b, w_down, final_norm):
    batch, seq, d = x.shape
    depth = w_in.shape[0]
    x2 = x.reshape(batch * seq, d)
    slopes = jnp.asarray(2.0 ** (-8.0 * (np.arange(ATT_HEADS) + 1.0) / ATT_HEADS), dtype=f32)
    w_main, w_dt, dtb_row, alog_row, dskip_row, norm_row = _prepare(w_in, ssd_dt_bias, ssd_a_log, ssd_d, ssd_norm)
    w_out_b, w_up_b, w_down_b = w_out.astype(bf16), w_up.astype(bf16), w_down.astype(bf16)
    rows = lambda a: a[:, None, :]
    for i in range(depth):
        qkv, pr, dt = _inproj(i, x2, rows(mix_norm), w_main, w_dt, ssd_conv_w, rows(ssd_conv_b), seq)
        att = _attention(i, qkv, rows(attn_norm), slopes, batch, seq)
        ssm = _ssd(i, pr, dt, dtb_row, alog_row, dskip_row, norm_row, batch, seq)
        x2 = _outproj(i, att, ssm, pr, x2, w_out_b, sc_conv_w, rows(sc_conv_b), rows(sc_norm), seq)
        x2 = _ffn(i, x2, rows(ffn_norm), w_up_b, ffn_conv_w, rows(ffn_conv_b), w_down_b,
                  final_norm[None, :], seq, final_norm=(i == depth - 1))
    return x2.reshape(batch, seq, d)
```

```python
import functools

import numpy as np
import jax
import jax.numpy as jnp
from jax import lax
from jax.experimental import pallas as pl
from jax.experimental.pallas import tpu as pltpu

f32 = jnp.float32
bf16 = jnp.bfloat16

EPS = 1e-6
NEG = -1e30
LOG2E = 1.4426950408889634

D_MODEL = 1024
HEAD_DIM = 64
ATT_HEADS = 8
ATT_WIDTH = 512
DILATED_BRANCHES = ((128, 1), (512, 4), (2048, 16))
ATT_HALF = 64
SSD_HEADS = 8
SSD_WIDTH = 512
SSD_GROUPS = 2
SSD_STATE = 128
SSD_CONV = 5
SSD_CHUNK = 128
CONV_WIDTH = 512
CONV_GROUPS = 8
D_MIX = 1536
D_FF = 2816
D_IN = 4624

VMEM_LIMIT = 56 * 1024 * 1024
LANES = 128

_NT = (((1,), (1,)), ((), ()))


def _layer_spec(layer, *shape, **kw):
    return pl.BlockSpec((None,) + shape, lambda *_: (layer,) + (0,) * len(shape), **kw)


def _cparams(n_axes):
    return pltpu.CompilerParams(
        dimension_semantics=("arbitrary",) * n_axes, vmem_limit_bytes=VMEM_LIMIT)


def _split2(x):
    hi = x.astype(bf16)
    lo = (x - hi.astype(f32)).astype(bf16)
    return hi, lo


def _dot(a, b):
    return jnp.dot(a, b, preferred_element_type=f32)


def _dot_parts(parts, mat):
    acc = _dot(parts[0], mat)
    for p in parts[1:]:
        acc = acc + _dot(p, mat)
    return acc


def _parts_dot(mat, parts):
    acc = _dot(mat, parts[0])
    for p in parts[1:]:
        acc = acc + _dot(mat, p)
    return acc


def _silu(x):
    return x * jax.nn.sigmoid(x)


def _group_ones(width, group):
    idx = np.arange(width) // group
    return jnp.asarray(idx[:, None] == idx[None, :], dtype=bf16)


IN_TM = 512
IN_CHUNK = 512
QKV_W = 3 * ATT_WIDTH
PR_W = 3072
DT_W = 256
XBC_LO, XBC_HI = 512, 1536
IN_HALO = 8


def _inproj_body(xp_ref, x_ref, xn_ref, g_ref, w_ref, wdt_ref, cw_ref, cb_ref,
                 qkv_ref, pr_ref, dt_ref, hn_ref, stage_ref, *, tiles_per_seq):
    i = pl.program_id(0)
    tm, h = IN_TM, IN_HALO
    keep_p = jnp.where(i % tiles_per_seq > 0, 1.0, 0.0)
    keep_n = jnp.where(i % tiles_per_seq < tiles_per_seq - 1, 1.0, 0.0)
    x = x_ref[...]
    ms = jnp.mean(x * x, axis=-1, keepdims=True)
    hn = ((x * lax.rsqrt(ms + EPS)) * g_ref[...]).astype(bf16)
    halo = jnp.concatenate([xp_ref[...] * keep_p, xn_ref[...] * keep_n], axis=0)
    msh = jnp.mean(halo * halo, axis=-1, keepdims=True)
    hh = ((halo * lax.rsqrt(msh + EPS)) * g_ref[...]).astype(bf16)
    hn_ref[0:2 * h, :] = hh
    hn_ref[2 * h:, :] = hn
    half = SSD_CONV // 2
    for slot, c0 in enumerate(range(XBC_LO, XBC_HI, IN_CHUNK)):
        u = _dot(hn_ref[...], w_ref[:, QKV_W + c0:QKV_W + c0 + IN_CHUNK])
        slabs = []
        for sl in range(IN_CHUNK // LANES):
            ls = slice(sl * LANES, (sl + 1) * LANES)
            cs = slice(c0 - XBC_LO + sl * LANES, c0 - XBC_LO + (sl + 1) * LANES)
            cur = u[2 * h:, ls]
            stage_ref[slot, sl, pl.ds(0, h, stride=2), :] = u[0:h, ls]
            stage_ref[slot, sl, pl.ds(2 * h, tm, stride=2), :] = cur
            stage_ref[slot, sl, pl.ds(2 * (h + tm), h, stride=2), :] = u[h:2 * h, ls]
            acc = cb_ref[:, cs] + cw_ref[half:half + 1, cs] * cur
            for kk in range(SSD_CONV):
                if kk != half:
                    acc = acc + cw_ref[kk:kk + 1, cs] * stage_ref[
                        slot, sl, pl.ds(2 * (h - half + kk), tm, stride=2), :]
            slabs.append(acc)
        pr_ref[:, c0:c0 + IN_CHUNK] = _silu(jnp.concatenate(slabs, axis=-1)).astype(bf16)
    for c0 in range(0, QKV_W, IN_CHUNK):
        qkv_ref[:, c0:c0 + IN_CHUNK] = _dot(hn, w_ref[:, c0:c0 + IN_CHUNK])
    for c0 in list(range(0, XBC_LO, IN_CHUNK)) + list(range(XBC_HI, PR_W, IN_CHUNK)):
        pr_ref[:, c0:c0 + IN_CHUNK] = _dot(
            hn, w_ref[:, QKV_W + c0:QKV_W + c0 + IN_CHUNK]).astype(bf16)
    dt_ref[...] = _dot(hn, wdt_ref[...])


def _inproj(layer, x2, g, w_main, w_dt, conv_w, conv_b, seq):
    m = x2.shape[0]
    tm, h = IN_TM, IN_HALO
    per = tm // h
    nb = m // h
    const = lambda i: (0, 0)
    return pl.pallas_call(
        functools.partial(_inproj_body, tiles_per_seq=seq // tm),
        grid=(m // IN_TM,),
        in_specs=[
            pl.BlockSpec((h, D_MODEL), lambda i: (jnp.maximum(i * per - 1, 0), 0)),
            pl.BlockSpec((IN_TM, D_MODEL), lambda i: (i, 0)),
            pl.BlockSpec((h, D_MODEL), lambda i: (jnp.minimum((i + 1) * per, nb - 1), 0)),
            _layer_spec(layer, 1, D_MODEL),
            _layer_spec(layer, D_MODEL, QKV_W + PR_W, pipeline_mode=pl.Buffered(1)),
            _layer_spec(layer, D_MODEL, DT_W, pipeline_mode=pl.Buffered(1)),
            _layer_spec(layer, SSD_CONV, XBC_HI - XBC_LO),
            _layer_spec(layer, 1, XBC_HI - XBC_LO),
        ],
        out_specs=[
            pl.BlockSpec((IN_TM, QKV_W), lambda i: (i, 0)),
            pl.BlockSpec((IN_TM, PR_W), lambda i: (i, 0)),
            pl.BlockSpec((IN_TM, DT_W), lambda i: (i, 0)),
        ],
        out_shape=[
            jax.ShapeDtypeStruct((m, QKV_W), f32),
            jax.ShapeDtypeStruct((m, PR_W), bf16),
            jax.ShapeDtypeStruct((m, DT_W), f32),
        ],
        scratch_shapes=[
            pltpu.VMEM((tm + 2 * h, D_MODEL), bf16),
            pltpu.VMEM(((XBC_HI - XBC_LO) // IN_CHUNK, IN_CHUNK // LANES, 2 * (tm + 2 * h), LANES), f32),
        ],
        compiler_params=_cparams(1),
        name="inproj",
    )(x2, x2, x2, g, w_main, w_dt, conv_w, conv_b)


ATT_TQ = 128
ATT_TK = ATT_TQ + 2 * ATT_HALF
ATT_FIN = 512
ATT_UNROLL = 32
ATT_OFFSETS = (0, -ATT_HALF, -2 * ATT_HALF)
ATT_PRE = 4


def _attn_body(slopes_ref, q_ref, k_ref, v_ref, g_ref, gmat_ref, o_ref,
               acc_ref, m_ref, l_ref, bias_ref, d4_ref, unperm_ref, *, seq):
    pair = pl.program_id(1)
    lane = lax.broadcasted_iota(jnp.int32, (1, 2 * HEAD_DIM), 1)
    head0 = lane < HEAD_DIM
    qscale = HEAD_DIM ** -0.5 * LOG2E
    qmask0 = jnp.where(head0, qscale, 0.0)
    qmask1 = jnp.where(head0, 0.0, qscale)

    row = lax.broadcasted_iota(jnp.int32, (2 * ATT_TQ, ATT_TK), 0)
    col = lax.broadcasted_iota(jnp.int32, (2 * ATT_TQ, ATT_TK), 1)
    rel0 = col - (row & (ATT_TQ - 1))
    slope_rows = jnp.where(row < ATT_TQ, slopes_ref[2 * pair], slopes_ref[2 * pair + 1])
    for bi, (_, r) in enumerate(DILATED_BRANCHES):
        for vi, off in enumerate(ATT_OFFSETS):
            dist = jnp.abs(rel0 + off)
            bias_ref[bi, vi] = jnp.where(
                dist <= ATT_HALF, dist.astype(f32) * (slope_rows * (-float(r) * LOG2E)), NEG)

    pre_len = seq // ATT_PRE

    def deinterleave(j, carry):
        for a, ref in enumerate((q_ref, k_ref, v_ref)):
            for c in range(ATT_PRE):
                d4_ref[a, pl.ds(c * pre_len + pl.multiple_of(j * ATT_FIN, ATT_FIN), ATT_FIN), :] = (
                    ref[pl.ds(c + ATT_PRE * pl.multiple_of(j * ATT_FIN, ATT_FIN), ATT_FIN, stride=ATT_PRE), :])
        return carry

    lax.fori_loop(0, pre_len // ATT_FIN, deinterleave, 0)

    for bi, (_, r) in enumerate(DILATED_BRANCHES):
        sub_len = seq // r
        tiles_per_class = sub_len // ATT_TQ

        def scores(t, r=r, sub_len=sub_len, tiles_per_class=tiles_per_class):
            cls = t // tiles_per_class
            t0 = (t % tiles_per_class) * ATT_TQ
            k0 = jnp.clip(t0 - ATT_HALF, 0, sub_len - ATT_TK)
            variant = jnp.where(t0 == 0, 0, jnp.where(t0 == sub_len - ATT_TQ, 2, 1))
            if r > ATT_PRE:
                base = (cls % ATT_PRE) * (seq // ATT_PRE) + cls // ATT_PRE
                q_rows = pl.ds(base + (r // ATT_PRE) * t0, ATT_TQ, stride=r // ATT_PRE)
                q = d4_ref[0, q_rows, :]
                k_rows = pl.ds(base + (r // ATT_PRE) * k0, ATT_TK, stride=r // ATT_PRE)
                k = d4_ref[1, k_rows, :]
            else:
                q_rows = pl.ds(cls + r * t0, ATT_TQ, stride=r)
                q = q_ref[q_rows, :]
                k_rows = pl.ds(cls + r * k0, ATT_TK, stride=r)
                k = k_ref[k_rows, :]
            q2 = jnp.concatenate([q * qmask0, q * qmask1], axis=0).astype(bf16)
            s = lax.dot_general(q2, k.astype(bf16), _NT, preferred_element_type=f32)
            return q_rows, k_rows, variant, s

        def softmax(unit, bi=bi):
            q_rows, k_rows, variant, s = unit
            s = s + bias_ref[bi, variant]
            m = jnp.max(s, axis=-1, keepdims=True)
            return q_rows, k_rows, jnp.exp2(s - m).astype(bf16), m

        def values(unit, bi=bi, r=r):
            q_rows, k_rows, p, m = unit
            v = d4_ref[2, k_rows, :] if r > ATT_PRE else v_ref[k_rows, :]
            o0 = _dot(p[0:ATT_TQ], jnp.where(head0, v, 1.0).astype(bf16))
            o1 = _dot(p[ATT_TQ:], jnp.where(head0, 1.0, v).astype(bf16))
            acc_ref[bi, q_rows, :] = jnp.where(head0, o0, o1)
            l_ref[bi, q_rows, :] = jnp.where(head0, o1, o0)
            m_ref[bi, q_rows, :] = jnp.where(head0, m[0:ATT_TQ], m[ATT_TQ:])

        def tile_group(g, carry):
            st_s, st_p = {}, {}
            for step in range(ATT_UNROLL + 2):
                if step < ATT_UNROLL:
                    st_s[step] = scores(g * ATT_UNROLL + step)
                if 0 <= step - 1 < ATT_UNROLL:
                    st_p[step - 1] = softmax(st_s.pop(step - 1))
                if 0 <= step - 2 < ATT_UNROLL:
                    values(st_p.pop(step - 2))
            return carry

        lax.fori_loop(0, seq // (ATT_TQ * ATT_UNROLL), tile_group, 0)

    gmat = gmat_ref[...]

    def combine(t, carry):
        r0 = pl.multiple_of(t * ATT_FIN, ATT_FIN)
        rows = pl.ds(r0, ATT_FIN)
        part = ATT_FIN // ATT_PRE

        def branch_rows(ref, slot, b):
            if DILATED_BRANCHES[b][1] <= ATT_PRE:
                return ref[b, rows, :]
            for c in range(ATT_PRE):
                unperm_ref[slot, pl.ds(c, part, stride=ATT_PRE), :] = (
                    ref[b, pl.ds(c * pre_len + pl.multiple_of(t * part, part), part), :])
            return unperm_ref[slot]

        ms_ = [branch_rows(m_ref, 0, b) for b in range(len(DILATED_BRANCHES))]
        mx = functools.reduce(jnp.maximum, ms_)
        num, den = None, None
        for b, mb in enumerate(ms_):
            e = jnp.exp2(mb - mx)
            nb = e * branch_rows(acc_ref, 1, b)
            db = e * pltpu.roll(branch_rows(l_ref, 2, b), HEAD_DIM, axis=1)
            num = nb if num is None else num + nb
            den = db if den is None else den + db
        y = num / den
        ms = _dot((y * y).astype(bf16), gmat) * (1.0 / HEAD_DIM)
        o_ref[rows, :] = (y * lax.rsqrt(ms + EPS) * g_ref[...]).astype(o_ref.dtype)
        return carry

    lax.fori_loop(0, seq // ATT_FIN, combine, 0)


def _attention(layer, qkv, attn_norm_row, slopes, batch, seq):
    m = qkv.shape[0]
    pairs = ATT_HEADS // 2
    w = 2 * HEAD_DIM
    nbr = len(DILATED_BRANCHES)
    return pl.pallas_call(
        functools.partial(_attn_body, seq=seq),
        grid=(batch, pairs),
        in_specs=[
            pl.BlockSpec(memory_space=pltpu.SMEM),
            pl.BlockSpec((seq, w), lambda b, p: (b, p)),
            pl.BlockSpec((seq, w), lambda b, p: (b, pairs + p)),
            pl.BlockSpec((seq, w), lambda b, p: (b, 2 * pairs + p)),
            pl.BlockSpec((None, 1, w), lambda b, p: (layer, 0, p)),
            pl.BlockSpec((w, w), lambda b, p: (0, 0)),
        ],
        out_specs=pl.BlockSpec((seq, w), lambda b, p: (b, p)),
        out_shape=jax.ShapeDtypeStruct((m, ATT_WIDTH), bf16),
        scratch_shapes=[
            pltpu.VMEM((nbr, seq, w), f32),
            pltpu.VMEM((nbr, seq, w), f32),
            pltpu.VMEM((nbr, seq, w), f32),
            pltpu.VMEM((nbr, len(ATT_OFFSETS), 2 * ATT_TQ, ATT_TK), f32),
            pltpu.VMEM((3, seq, w), f32),
            pltpu.VMEM((3, ATT_FIN, w), f32),
        ],
        compiler_params=_cparams(2),
        name="dilated_attention",
    )(slopes, qkv, qkv, qkv, attn_norm_row, _group_ones(w, HEAD_DIM))


SSD_L = SSD_CHUNK
SSD_GW = SSD_WIDTH // SSD_GROUPS
SSD_E = SSD_HEADS // SSD_GROUPS
SSD_UA = 4
SSD_UB = 8


def _ssd_body(z_ref, xs_ref, b_ref, c_ref, dt_ref, dtb_ref, alog_ref,
              dsk_ref, ng_ref, expand_ref, o_ref,
              bmt_ref, y_ref, ecb_ref, xinb_ref, decb_ref, st_ref, *, seq):
    n_chunks = seq // SSD_L
    L = SSD_L
    row_i = lax.broadcasted_iota(jnp.int32, (L, L), 0)
    col_i = lax.broadcasted_iota(jnp.int32, (L, L), 1)
    lower = row_i >= col_i
    upper = row_i <= col_i
    lower_b = lower.astype(bf16)
    upper_b = upper.astype(bf16)
    lane8 = lax.broadcasted_iota(jnp.int32, (1, L), 1)
    fwd_lane = lane8 < SSD_E
    a_row = -jnp.exp(alog_ref[...])
    expand = expand_ref[...]
    lane_blk = lax.broadcasted_iota(jnp.int32, (1, SSD_GW), 1) // 64


    def stage1(c):
        rows = pl.ds(pl.multiple_of(c * L, L), L)
        d = dict(c=c, rows=rows, xs=xs_ref[rows, :].astype(f32), bm=b_ref[rows, :], cm=c_ref[rows, :])
        dtv = jax.nn.softplus(dt_ref[rows, :] + dtb_ref[...])
        adt = dtv * a_row
        parts = _split2(adt)
        d["cum"] = jnp.where(fwd_lane, _parts_dot(lower_b, parts), _parts_dot(upper_b, parts))
        d["tot"] = jnp.sum(adt, axis=0, keepdims=True)
        d["dtv"] = dtv
        return d

    def stage2(d):
        cum, tot, dtv = d["cum"], d["tot"], d["dtv"]
        d["src_t"] = (cum - jnp.log(dtv)).T
        d["gram"] = lax.dot_general(d["cm"], d["bm"], _NT, preferred_element_type=f32)
        d["bm_t"] = d["bm"].T
        bmt_ref[d["c"]] = d["bm_t"]
        dec_hi, dec_lo = _split2(jnp.broadcast_to(jnp.exp(tot), (8, L)))
        wide = _dot(jnp.concatenate(
            [jnp.exp(cum).astype(bf16), (jnp.exp(tot - cum) * dtv).astype(bf16), dec_hi, dec_lo], axis=0), expand)
        d["ecum_x"], d["w_x"] = wide[0:L], wide[L:2 * L]
        d["dec"] = wide[2 * L:2 * L + 8] + wide[2 * L + 8:]
        return d

    def stage3(d):
        xs, cum, src_t, gram = d["xs"], d["cum"], d["src_t"], d["gram"]
        m_rows = []
        for e in range(SSD_E):
            df = jnp.exp(jnp.where(lower, cum[:, e:e + 1] - src_t[e:e + 1, :], NEG))
            db = jnp.exp(jnp.where(upper, cum[:, SSD_E + e:SSD_E + e + 1]
                                   - src_t[SSD_E + e:SSD_E + e + 1, :], NEG))
            m_rows.append((gram * (df + db)).astype(bf16))
        y_all = _dot(jnp.concatenate(m_rows, axis=0), xs_ref[d["rows"], :])
        y = dsk_ref[...] * xs
        for e in range(SSD_E):
            y = jnp.where(lane_blk == e, y_all[e * L:(e + 1) * L, :] + y, y)
        d["y"] = y
        d["cs_f"] = _dot(d["bm_t"], (xs * d["w_x"][:, 0:SSD_GW]).astype(bf16))
        xinb_ref[d["rows"], :] = (xs * d["w_x"][:, SSD_GW:]).astype(bf16)
        ecb_ref[d["rows"], :] = d["ecum_x"][:, SSD_GW:]
        decb_ref[d["c"]] = d["dec"][:, SSD_GW:]
        return d

    st_ref[...] = jnp.zeros_like(st_ref)

    def pass_a(i, carry):
        ds = [stage1(i * SSD_UA + u) for u in range(SSD_UA)]
        ds = [stage2(d) for d in ds]
        ds = [stage3(d) for d in ds]
        state = st_ref[...]
        for d in ds:
            y_ref[d["rows"], :] = d["y"] + _dot(d["cm"], state.astype(bf16)) * d["ecum_x"][:, 0:SSD_GW]
            state = state * d["dec"][0:1, 0:SSD_GW] + d["cs_f"]
        st_ref[...] = state
        return carry

    lax.fori_loop(0, n_chunks // SSD_UA, pass_a, 0)

    st_ref[...] = jnp.zeros_like(st_ref)

    def pass_b(i, carry):
        chunks = [n_chunks - 1 - (i * SSD_UB + u) for u in range(SSD_UB)]
        rows = [pl.ds(pl.multiple_of(c * L, L), L) for c in chunks]
        cs = [_dot(bmt_ref[c], xinb_ref[r, :]) for c, r in zip(chunks, rows)]
        state = st_ref[...]
        ys = []
        for c, r, cs_b in zip(chunks, rows, cs):
            ys.append(y_ref[r, :] + _dot(c_ref[r, :], state.astype(bf16)) * ecb_ref[r, :])
            state = state * decb_ref[c][0:1, :] + cs_b
        st_ref[...] = state
        for r, y in zip(rows, ys):
            y = y * _silu(z_ref[r, :].astype(f32))
            ms = jnp.mean(y * y, axis=-1, keepdims=True)
            o_ref[r, :] = (y * lax.rsqrt(ms + EPS) * ng_ref[...]).astype(o_ref.dtype)
        return carry

    lax.fori_loop(0, n_chunks // SSD_UB, pass_b, 0)


def _ssd(layer, pr, dt, dtb_row, alog_row, dskip_row, norm_row, batch, seq):
    m = pr.shape[0]
    expand = np.zeros((SSD_L, 2 * SSD_GW), np.float32)
    for e in range(SSD_E):
        expand[e, 64 * e:64 * (e + 1)] = 1.0
        expand[SSD_E + e, SSD_GW + 64 * e:SSD_GW + 64 * (e + 1)] = 1.0
    expand = jnp.asarray(expand, dtype=bf16)
    return pl.pallas_call(
        functools.partial(_ssd_body, seq=seq),
        grid=(batch, SSD_GROUPS),
        in_specs=[
            pl.BlockSpec((seq, SSD_GW), lambda b, g: (b, g)),
            pl.BlockSpec((seq, SSD_GW), lambda b, g: (b, 2 + g)),
            pl.BlockSpec((seq, SSD_STATE), lambda b, g: (b, 8 + g)),
            pl.BlockSpec((seq, SSD_STATE), lambda b, g: (b, 10 + g)),
            pl.BlockSpec((seq, 128), lambda b, g: (b, g)),
            pl.BlockSpec((None, None, 1, 128), lambda b, g: (layer, g, 0, 0)),
            pl.BlockSpec((None, None, 1, 128), lambda b, g: (layer, g, 0, 0)),
            pl.BlockSpec((None, None, 1, SSD_GW), lambda b, g: (layer, g, 0, 0)),
            pl.BlockSpec((None, None, 1, SSD_GW), lambda b, g: (layer, g, 0, 0)),
            pl.BlockSpec((SSD_L, 2 * SSD_GW), lambda b, g: (0, 0)),
        ],
        out_specs=pl.BlockSpec((seq, SSD_GW), lambda b, g: (b, g)),
        out_shape=jax.ShapeDtypeStruct((m, SSD_WIDTH), bf16),
        scratch_shapes=[
            pltpu.VMEM((seq // SSD_L, SSD_STATE, SSD_L), bf16),
            pltpu.VMEM((seq, SSD_GW), f32),
            pltpu.VMEM((seq, SSD_GW), f32),
            pltpu.VMEM((seq, SSD_GW), bf16),
            pltpu.VMEM((seq // SSD_L, 8, SSD_GW), f32),
            pltpu.VMEM((SSD_STATE, SSD_GW), f32),
        ],
        compiler_params=_cparams(2),
        name="ssd",
    )(pr, pr, pr, pr, dt, dtb_row, alog_row, dskip_row, norm_row, expand)


OUT_TM = 512
OUT_HALO = 16


def _outproj_body(att_ref, ssm_ref, gb_ref, gcp_ref, gc_ref, gcn_ref, hcp_ref, hc_ref, hcn_ref,
                  x_ref, w_ref, cw_ref, cb_ref, ng_ref, gmat_ref, o_ref, stage_ref, *, tiles_per_seq):
    i = pl.program_id(0)
    keep_p = jnp.where(i % tiles_per_seq > 0, 1.0, 0.0)
    keep_n = jnp.where(i % tiles_per_seq < tiles_per_seq - 1, 1.0, 0.0)
    tm = OUT_TM
    h = OUT_HALO
    ready = (_dot(att_ref[...], w_ref[0:ATT_WIDTH, :])
             + _dot(ssm_ref[...], w_ref[ATT_WIDTH:ATT_WIDTH + SSD_WIDTH, :]))
    slabs = []
    for sl in range(CONV_WIDTH // LANES):
        cs = slice(sl * LANES, (sl + 1) * LANES)
        cur = gc_ref[:, cs].astype(f32) * hc_ref[:, cs].astype(f32)
        stage_ref[sl, pl.ds(0, h, stride=2), :] = gcp_ref[:, cs].astype(f32) * hcp_ref[:, cs].astype(f32) * keep_p
        stage_ref[sl, pl.ds(2 * h, tm, stride=2), :] = cur
        stage_ref[sl, pl.ds(2 * (h + tm), h, stride=2), :] = (
            gcn_ref[:, cs].astype(f32) * hcn_ref[:, cs].astype(f32) * keep_n)
        slabs.append(cb_ref[:, cs] + cw_ref[0:1, cs] * stage_ref[sl, pl.ds(2 * (h - 1), tm, stride=2), :]
                     + cw_ref[1:2, cs] * cur
                     + cw_ref[2:3, cs] * stage_ref[sl, pl.ds(2 * (h + 1), tm, stride=2), :])
    y = gb_ref[...].astype(f32) * jnp.concatenate(slabs, axis=-1)
    ms = _dot((y * y).astype(bf16), gmat_ref[...]) * (1.0 / (CONV_WIDTH // CONV_GROUPS))
    sc = (y * lax.rsqrt(ms + EPS) * ng_ref[...]).astype(bf16)
    o_ref[...] = x_ref[...] + ready + _dot(sc, w_ref[ATT_WIDTH + SSD_WIDTH:, :])


def _outproj(layer, att, ssm, pr, x2, w_out, sc_w, sc_b, sc_g, seq):
    m = x2.shape[0]
    tm, h = OUT_TM, OUT_HALO
    per = tm // h
    nb = m // h
    const = lambda i: (0, 0)
    prev = lambda col: (lambda i: (jnp.maximum(i * per - 1, 0), col))
    nxt = lambda col: (lambda i: (jnp.minimum((i + 1) * per, nb - 1), col))
    return pl.pallas_call(
        functools.partial(_outproj_body, tiles_per_seq=seq // tm),
        grid=(m // tm,),
        in_specs=[
            pl.BlockSpec((tm, ATT_WIDTH), lambda i: (i, 0)),
            pl.BlockSpec((tm, SSD_WIDTH), lambda i: (i, 0)),
            pl.BlockSpec((tm, CONV_WIDTH), lambda i: (i, 3)),
            pl.BlockSpec((h, CONV_WIDTH), prev(4)),
            pl.BlockSpec((tm, CONV_WIDTH), lambda i: (i, 4)),
            pl.BlockSpec((h, CONV_WIDTH), nxt(4)),
            pl.BlockSpec((h, CONV_WIDTH), prev(5)),
            pl.BlockSpec((tm, CONV_WIDTH), lambda i: (i, 5)),
            pl.BlockSpec((h, CONV_WIDTH), nxt(5)),
            pl.BlockSpec((tm, D_MODEL), lambda i: (i, 0)),
            _layer_spec(layer, D_MIX, D_MODEL, pipeline_mode=pl.Buffered(1)),
            _layer_spec(layer, 3, CONV_WIDTH),
            _layer_spec(layer, 1, CONV_WIDTH),
            _layer_spec(layer, 1, CONV_WIDTH),
            pl.BlockSpec((CONV_WIDTH, CONV_WIDTH), const),
        ],
        out_specs=pl.BlockSpec((tm, D_MODEL), lambda i: (i, 0)),
        out_shape=jax.ShapeDtypeStruct((m, D_MODEL), f32),
        scratch_shapes=[pltpu.VMEM((CONV_WIDTH // LANES, 2 * (tm + 2 * h), LANES), f32)],
        compiler_params=_cparams(1),
        name="outproj",
    )(att, ssm, pr, pr, pr, pr, pr, pr, pr, x2, w_out, sc_w, sc_b, sc_g,
      _group_ones(CONV_WIDTH, CONV_WIDTH // CONV_GROUPS))


FFN_TM = 512
FFN_HALO = 8
FFN_CHUNK = 256


def _ffn_body(xp_ref, x_ref, xn_ref, g_ref, wup_ref, cw_ref, cb_ref, wdn_ref, fg_ref, o_ref,
              hn_ref, u_ref, act_ref, *, tiles_per_seq, final_norm):
    i = pl.program_id(0)
    tm, h = FFN_TM, FFN_HALO
    keep_p = jnp.where(i % tiles_per_seq > 0, 1.0, 0.0)
    keep_n = jnp.where(i % tiles_per_seq < tiles_per_seq - 1, 1.0, 0.0)
    xall = jnp.concatenate([xp_ref[...] * keep_p, x_ref[...], xn_ref[...] * keep_n], axis=0)
    ms = jnp.mean(xall * xall, axis=-1, keepdims=True)
    hn_ref[...] = ((xall * lax.rsqrt(ms + EPS)) * g_ref[...]).astype(bf16)
    rows = tm + 2 * h
    for c0 in range(0, D_FF, FFN_CHUNK):
        halves = []
        for part in range(2):
            col = part * D_FF + c0
            u = _dot(hn_ref[...], wup_ref[:, col:col + FFN_CHUNK])
            slabs = []
            for sl in range(FFN_CHUNK // LANES):
                u_ref[part, sl, pl.ds(0, rows, stride=2), :] = u[:, sl * LANES:(sl + 1) * LANES]
                cs = slice(col + sl * LANES, col + (sl + 1) * LANES)
                slabs.append(cb_ref[:, cs]
                             + cw_ref[0:1, cs] * u_ref[part, sl, pl.ds(2 * (h - 1), tm, stride=2), :]
                             + cw_ref[1:2, cs] * u[h:h + tm, sl * LANES:(sl + 1) * LANES]
                             + cw_ref[2:3, cs] * u_ref[part, sl, pl.ds(2 * (h + 1), tm, stride=2), :])
            halves.append(jnp.concatenate(slabs, axis=-1))
        act_ref[:, c0:c0 + FFN_CHUNK] = (_silu(halves[0]) * halves[1]).astype(bf16)
    out = x_ref[...] + _dot(act_ref[...], wdn_ref[...])
    if final_norm:
        ms2 = jnp.mean(out * out, axis=-1, keepdims=True)
        out = (out * lax.rsqrt(ms2 + EPS)) * fg_ref[...]
    o_ref[...] = out


def _ffn(layer, x2, g, w_up, conv_w, conv_b, w_down, final_g, seq, final_norm):
    m = x2.shape[0]
    tm, h = FFN_TM, FFN_HALO
    per = tm // h
    nb = m // h
    const = lambda i: (0, 0)
    return pl.pallas_call(
        functools.partial(_ffn_body, tiles_per_seq=seq // tm, final_norm=final_norm),
        grid=(m // tm,),
        in_specs=[
            pl.BlockSpec((h, D_MODEL), lambda i: (jnp.maximum(i * per - 1, 0), 0)),
            pl.BlockSpec((tm, D_MODEL), lambda i: (i, 0)),
            pl.BlockSpec((h, D_MODEL), lambda i: (jnp.minimum((i + 1) * per, nb - 1), 0)),
            _layer_spec(layer, 1, D_MODEL),
            _layer_spec(layer, D_MODEL, 2 * D_FF, pipeline_mode=pl.Buffered(1)),
            _layer_spec(layer, 3, 2 * D_FF),
            _layer_spec(layer, 1, 2 * D_FF),
            _layer_spec(layer, D_FF, D_MODEL, pipeline_mode=pl.Buffered(1)),
            pl.BlockSpec((1, D_MODEL), const),
        ],
        out_specs=pl.BlockSpec((tm, D_MODEL), lambda i: (i, 0)),
        out_shape=jax.ShapeDtypeStruct((m, D_MODEL), f32),
        scratch_shapes=[
            pltpu.VMEM((tm + 2 * h, D_MODEL), bf16),
            pltpu.VMEM((2, FFN_CHUNK // LANES, 2 * (tm + 2 * h), LANES), f32),
            pltpu.VMEM((tm, D_FF), bf16),
        ],
        compiler_params=_cparams(1),
        name="convffn",
    )(x2, x2, x2, g, w_up, conv_w, conv_b, w_down, final_g)


def _prepare(w_in, ssd_dt_bias, ssd_a_log, ssd_d, ssd_norm):
    depth = w_in.shape[0]
    dt0 = 3 * ATT_WIDTH + SSD_WIDTH + SSD_WIDTH + 2 * SSD_GROUPS * SSD_STATE
    w_main = jnp.concatenate([w_in[:, :, :dt0], w_in[:, :, dt0 + 2 * SSD_HEADS:]], axis=2).astype(bf16)
    w_dt_cols = w_in[:, :, dt0:dt0 + 2 * SSD_HEADS]

    def per_group_lanes(v):
        v = v.reshape(depth, 2, SSD_GROUPS, SSD_E).transpose(0, 2, 1, 3).reshape(depth, SSD_GROUPS, 1, 2 * SSD_E)
        return jnp.pad(v, ((0, 0), (0, 0), (0, 0), (0, 128 - 2 * SSD_E)))

    wd = w_dt_cols.reshape(depth, D_MODEL, 2, SSD_GROUPS, SSD_E).transpose(0, 1, 3, 2, 4)
    wd = wd.reshape(depth, D_MODEL, SSD_GROUPS, 2 * SSD_E)
    w_dt = jnp.pad(wd, ((0, 0), (0, 0), (0, 0), (0, 128 - 2 * SSD_E))).reshape(depth, D_MODEL, DT_W).astype(bf16)
    dtb_row = per_group_lanes(ssd_dt_bias)
    alog_row = per_group_lanes(ssd_a_log)
    dskip_row = jnp.repeat(ssd_d, HEAD_DIM, axis=-1).reshape(depth, SSD_GROUPS, 1, SSD_GW)
    norm_row = ssd_norm.reshape(depth, SSD_GROUPS, 1, SSD_GW)
    return w_main, w_dt, dtb_row, alog_row, dskip_row, norm_row


def kernel(x, mix_norm, w_in, ssd_conv_w, ssd_conv_b, ssd_dt_bias, ssd_a_log, ssd_d, ssd_norm, sc_conv_w, sc_conv_b, attn_norm, sc_norm, w_out, ffn_norm, w_up, ffn_conv_w, ffn_conv_b, w_down, final_norm):
    batch, seq, d = x.shape
    depth = w_in.shape[0]
    x2 = x.reshape(batch * seq, d)
    slopes = jnp.asarray(2.0 ** (-8.0 * (np.arange(ATT_HEADS) + 1.0) / ATT_HEADS), dtype=f32)
    w_main, w_dt, dtb_row, alog_row, dskip_row, norm_row = _prepare(w_in, ssd_dt_bias, ssd_a_log, ssd_d, ssd_norm)
    w_out_b, w_up_b, w_down_b = w_out.astype(bf16), w_up.astype(bf16), w_down.astype(bf16)
    rows = lambda a: a[:, None, :]
    for i in range(depth):
        qkv, pr, dt = _inproj(i, x2, rows(mix_norm), w_main, w_dt, ssd_conv_w, rows(ssd_conv_b), seq)
        att = _attention(i, qkv, rows(attn_norm), slopes, batch, seq)
        ssm = _ssd(i, pr, dt, dtb_row, alog_row, dskip_row, norm_row, batch, seq)
        x2 = _outproj(i, att, ssm, pr, x2, w_out_b, sc_conv_w, rows(sc_conv_b), rows(sc_norm), seq)
        x2 = _ffn(i, x2, rows(ffn_norm), w_up_b, ffn_conv_w, rows(ffn_conv_b), w_down_b,
                  final_norm[None, :], seq, final_norm=(i == depth - 1))
    return x2.reshape(batch, seq, d)
```

```python
import functools

import numpy as np
import jax
import jax.numpy as jnp
from jax import lax
from jax.experimental import pallas as pl
from jax.experimental.pallas import tpu as pltpu

f32 = jnp.float32
bf16 = jnp.bfloat16

EPS = 1e-6
NEG = -1e30
LOG2E = 1.4426950408889634

D_MODEL = 1024
HEAD_DIM = 64
ATT_HEADS = 8
ATT_WIDTH = 512
DILATED_BRANCHES = ((128, 1), (512, 4), (2048, 16))
ATT_HALF = 64
SSD_HEADS = 8
SSD_WIDTH = 512
SSD_GROUPS = 2
SSD_STATE = 128
SSD_CONV = 5
SSD_CHUNK = 128
CONV_WIDTH = 512
CONV_GROUPS = 8
D_MIX = 1536
D_FF = 2816
D_IN = 4624

VMEM_LIMIT = 56 * 1024 * 1024
LANES = 128

_NT = (((1,), (1,)), ((), ()))


def _layer_spec(layer, *shape, **kw):
    return pl.BlockSpec((None,) + shape, lambda *_: (layer,) + (0,) * len(shape), **kw)


def _cparams(n_axes):
    return pltpu.CompilerParams(
        dimension_semantics=("arbitrary",) * n_axes, vmem_limit_bytes=VMEM_LIMIT)


def _split2(x):
    hi = x.astype(bf16)
    lo = (x - hi.astype(f32)).astype(bf16)
    return hi, lo


def _dot(a, b):
    return jnp.dot(a, b, preferred_element_type=f32)


def _dot_parts(parts, mat):
    acc = _dot(parts[0], mat)
    for p in parts[1:]:
        acc = acc + _dot(p, mat)
    return acc


def _parts_dot(mat, parts):
    acc = _dot(mat, parts[0])
    for p in parts[1:]:
        acc = acc + _dot(mat, p)
    return acc


def _silu(x):
    return x * jax.nn.sigmoid(x)


def _group_ones(width, group):
    idx = np.arange(width) // group
    return jnp.asarray(idx[:, None] == idx[None, :], dtype=bf16)


IN_TM = 512
IN_CHUNK = 512
QKV_W = 3 * ATT_WIDTH
PR_W = 3072
DT_W = 256
XBC_LO, XBC_HI = 512, 1536
IN_HALO = 8


def _inproj_body(xp_ref, x_ref, xn_ref, g_ref, w_ref, wdt_ref, cw_ref, cb_ref,
                 qkv_ref, pr_ref, dt_ref, hn_ref, stage_ref, *, tiles_per_seq):
    i = pl.program_id(0)
    tm, h = IN_TM, IN_HALO
    keep_p = jnp.where(i % tiles_per_seq > 0, 1.0, 0.0)
    keep_n = jnp.where(i % tiles_per_seq < tiles_per_seq - 1, 1.0, 0.0)
    x = x_ref[...]
    ms = jnp.mean(x * x, axis=-1, keepdims=True)
    hn = ((x * lax.rsqrt(ms + EPS)) * g_ref[...]).astype(bf16)
    halo = jnp.concatenate([xp_ref[...] * keep_p, xn_ref[...] * keep_n], axis=0)
    msh = jnp.mean(halo * halo, axis=-1, keepdims=True)
    hh = ((halo * lax.rsqrt(msh + EPS)) * g_ref[...]).astype(bf16)
    hn_ref[0:2 * h, :] = hh
    hn_ref[2 * h:, :] = hn
    half = SSD_CONV // 2
    for slot, c0 in enumerate(range(XBC_LO, XBC_HI, IN_CHUNK)):
        u = _dot(hn_ref[...], w_ref[:, QKV_W + c0:QKV_W + c0 + IN_CHUNK])
        slabs = []
        for sl in range(IN_CHUNK // LANES):
            ls = slice(sl * LANES, (sl + 1) * LANES)
            cs = slice(c0 - XBC_LO + sl * LANES, c0 - XBC_LO + (sl + 1) * LANES)
            cur = u[2 * h:, ls]
            stage_ref[slot, sl, pl.ds(0, h, stride=2), :] = u[0:h, ls]
            stage_ref[slot, sl, pl.ds(2 * h, tm, stride=2), :] = cur
            stage_ref[slot, sl, pl.ds(2 * (h + tm), h, stride=2), :] = u[h:2 * h, ls]
            acc = cb_ref[:, cs] + cw_ref[half:half + 1, cs] * cur
            for kk in range(SSD_CONV):
                if kk != half:
                    acc = acc + cw_ref[kk:kk + 1, cs] * stage_ref[
                        slot, sl, pl.ds(2 * (h - half + kk), tm, stride=2), :]
            slabs.append(acc)
        pr_ref[:, c0:c0 + IN_CHUNK] = _silu(jnp.concatenate(slabs, axis=-1)).astype(bf16)
    for c0 in range(0, QKV_W, IN_CHUNK):
        qkv_ref[:, c0:c0 + IN_CHUNK] = _dot(hn, w_ref[:, c0:c0 + IN_CHUNK])
    for c0 in list(range(0, XBC_LO, IN_CHUNK)) + list(range(XBC_HI, PR_W, IN_CHUNK)):
        pr_ref[:, c0:c0 + IN_CHUNK] = _dot(
            hn, w_ref[:, QKV_W + c0:QKV_W + c0 + IN_CHUNK]).astype(bf16)
    dt_ref[...] = _dot(hn, wdt_ref[...])


def _inproj(layer, x2, g, w_main, w_dt, conv_w, conv_b, seq):
    m = x2.shape[0]
    tm, h = IN_TM, IN_HALO
    per = tm // h
    nb = m // h
    const = lambda i: (0, 0)
    return pl.pallas_call(
        functools.partial(_inproj_body, tiles_per_seq=seq // tm),
        grid=(m // IN_TM,),
        in_specs=[
            pl.BlockSpec((h, D_MODEL), lambda i: (jnp.maximum(i * per - 1, 0), 0)),
            pl.BlockSpec((IN_TM, D_MODEL), lambda i: (i, 0)),
            pl.BlockSpec((h, D_MODEL), lambda i: (jnp.minimum((i + 1) * per, nb - 1), 0)),
            _layer_spec(layer, 1, D_MODEL),
            _layer_spec(layer, D_MODEL, QKV_W + PR_W, pipeline_mode=pl.Buffered(1)),
            _layer_spec(layer, D_MODEL, DT_W, pipeline_mode=pl.Buffered(1)),
            _layer_spec(layer, SSD_CONV, XBC_HI - XBC_LO),
            _layer_spec(layer, 1, XBC_HI - XBC_LO),
        ],
        out_specs=[
            pl.BlockSpec((IN_TM, QKV_W), lambda i: (i, 0)),
            pl.BlockSpec((IN_TM, PR_W), lambda i: (i, 0)),
            pl.BlockSpec((IN_TM, DT_W), lambda i: (i, 0)),
        ],
        out_shape=[
            jax.ShapeDtypeStruct((m, QKV_W), f32),
            jax.ShapeDtypeStruct((m, PR_W), bf16),
            jax.ShapeDtypeStruct((m, DT_W), f32),
        ],
        scratch_shapes=[
            pltpu.VMEM((tm + 2 * h, D_MODEL), bf16),
            pltpu.VMEM(((XBC_HI - XBC_LO) // IN_CHUNK, IN_CHUNK // LANES, 2 * (tm + 2 * h), LANES), f32),
        ],
        compiler_params=_cparams(1),
        name="inproj",
    )(x2, x2, x2, g, w_main, w_dt, conv_w, conv_b)


ATT_TQ = 128
ATT_TK = ATT_TQ + 2 * ATT_HALF
ATT_FIN = 512
ATT_UNROLL = 32
ATT_OFFSETS = (0, -ATT_HALF, -2 * ATT_HALF)
ATT_PRE = 4


def _attn_body(slopes_ref, q_ref, k_ref, v_ref, g_ref, gmat_ref, o_ref,
               acc_ref, m_ref, l_ref, bias_ref, d4_ref, unperm_ref, *, seq):
    pair = pl.program_id(1)
    lane = lax.broadcasted_iota(jnp.int32, (1, 2 * HEAD_DIM), 1)
    head0 = lane < HEAD_DIM
    qscale = HEAD_DIM ** -0.5 * LOG2E
    qmask0 = jnp.where(head0, qscale, 0.0)
    qmask1 = jnp.where(head0, 0.0, qscale)

    row = lax.broadcasted_iota(jnp.int32, (2 * ATT_TQ, ATT_TK), 0)
    col = lax.broadcasted_iota(jnp.int32, (2 * ATT_TQ, ATT_TK), 1)
    rel0 = col - (row & (ATT_TQ - 1))
    slope_rows = jnp.where(row < ATT_TQ, slopes_ref[2 * pair], slopes_ref[2 * pair + 1])
    for bi, (_, r) in enumerate(DILATED_BRANCHES):
        for vi, off in enumerate(ATT_OFFSETS):
            dist = jnp.abs(rel0 + off)
            bias_ref[bi, vi] = jnp.where(
                dist <= ATT_HALF, dist.astype(f32) * (slope_rows * (-float(r) * LOG2E)), NEG)

    pre_len = seq // ATT_PRE

    def deinterleave(j, carry):
        for a, ref in enumerate((q_ref, k_ref, v_ref)):
            for c in range(ATT_PRE):
                d4_ref[a, pl.ds(c * pre_len + pl.multiple_of(j * ATT_FIN, ATT_FIN), ATT_FIN), :] = (
                    ref[pl.ds(c + ATT_PRE * pl.multiple_of(j * ATT_FIN, ATT_FIN), ATT_FIN, stride=ATT_PRE), :])
        return carry

    lax.fori_loop(0, pre_len // ATT_FIN, deinterleave, 0)

    stages = []
    for bi, (_, r) in enumerate(DILATED_BRANCHES):
        sub_len = seq // r
        tiles_per_class = sub_len // ATT_TQ

        def scores(t, r=r, sub_len=sub_len, tiles_per_class=tiles_per_class):
            cls = t // tiles_per_class
            t0 = (t % tiles_per_class) * ATT_TQ
            k0 = jnp.clip(t0 - ATT_HALF, 0, sub_len - ATT_TK)
            variant = jnp.where(t0 == 0, 0, jnp.where(t0 == sub_len - ATT_TQ, 2, 1))
            if r > ATT_PRE:
                base = (cls % ATT_PRE) * (seq // ATT_PRE) + cls // ATT_PRE
                q_rows = pl.ds(base + (r // ATT_PRE) * t0, ATT_TQ, stride=r // ATT_PRE)
                q = d4_ref[0, q_rows, :]
                k_rows = pl.ds(base + (r // ATT_PRE) * k0, ATT_TK, stride=r // ATT_PRE)
                k = d4_ref[1, k_rows, :]
            else:
                q_rows = pl.ds(cls + r * t0, ATT_TQ, stride=r)
                q = q_ref[q_rows, :]
                k_rows = pl.ds(cls + r * k0, ATT_TK, stride=r)
                k = k_ref[k_rows, :]
            q2 = jnp.concatenate([q * qmask0, q * qmask1], axis=0).astype(bf16)
            s = lax.dot_general(q2, k.astype(bf16), _NT, preferred_element_type=f32)
            return q_rows, k_rows, variant, s

        def softmax(unit, bi=bi):
            q_rows, k_rows, variant, s = unit
            s = s + bias_ref[bi, variant]
            m = jnp.max(s, axis=-1, keepdims=True)
            return q_rows, k_rows, jnp.exp2(s - m).astype(bf16), m

        def values(unit, bi=bi, r=r):
            q_rows, k_rows, p, m = unit
            v = d4_ref[2, k_rows, :] if r > ATT_PRE else v_ref[k_rows, :]
            o0 = _dot(p[0:ATT_TQ], jnp.where(head0, v, 1.0).astype(bf16))
            o1 = _dot(p[ATT_TQ:], jnp.where(head0, 1.0, v).astype(bf16))
            acc_ref[bi, q_rows, :] = jnp.where(head0, o0, o1)
            l_ref[bi, q_rows, :] = jnp.where(head0, o1, o0)
            m_ref[bi, q_rows, :] = jnp.where(head0, m[0:ATT_TQ], m[ATT_TQ:])

        stages.append((scores, softmax, values))

    def tile_group(g, carry):
        units = [(bi, g * ATT_UNROLL + u) for bi in range(len(stages)) for u in range(ATT_UNROLL)]
        st_s, st_p = {}, {}
        for step in range(len(units) + 2):
            if step < len(units):
                bi, t = units[step]
                st_s[step] = stages[bi][0](t)
            if 0 <= step - 1 < len(units):
                st_p[step - 1] = stages[units[step - 1][0]][1](st_s.pop(step - 1))
            if 0 <= step - 2 < len(units):
                stages[units[step - 2][0]][2](st_p.pop(step - 2))
        return carry

    lax.fori_loop(0, seq // (ATT_TQ * ATT_UNROLL), tile_group, 0)

    gmat = gmat_ref[...]

    def combine(t, carry):
        r0 = pl.multiple_of(t * ATT_FIN, ATT_FIN)
        rows = pl.ds(r0, ATT_FIN)
        part = ATT_FIN // ATT_PRE

        def branch_rows(ref, slot, b):
            if DILATED_BRANCHES[b][1] <= ATT_PRE:
                return ref[b, rows, :]
            for c in range(ATT_PRE):
                unperm_ref[slot, pl.ds(c, part, stride=ATT_PRE), :] = (
                    ref[b, pl.ds(c * pre_len + pl.multiple_of(t * part, part), part), :])
            return unperm_ref[slot]

        ms_ = [branch_rows(m_ref, 0, b) for b in range(len(DILATED_BRANCHES))]
        mx = functools.reduce(jnp.maximum, ms_)
        num, den = None, None
        for b, mb in enumerate(ms_):
            e = jnp.exp2(mb - mx)
            nb = e * branch_rows(acc_ref, 1, b)
            db = e * pltpu.roll(branch_rows(l_ref, 2, b), HEAD_DIM, axis=1)
            num = nb if num is None else num + nb
            den = db if den is None else den + db
        y = num / den
        ms = _dot((y * y).astype(bf16), gmat) * (1.0 / HEAD_DIM)
        o_ref[rows, :] = (y * lax.rsqrt(ms + EPS) * g_ref[...]).astype(o_ref.dtype)
        return carry

    lax.fori_loop(0, seq // ATT_FIN, combine, 0)


def _attention(layer, qkv, attn_norm_row, slopes, batch, seq):
    m = qkv.shape[0]
    pairs = ATT_HEADS // 2
    w = 2 * HEAD_DIM
    nbr = len(DILATED_BRANCHES)
    return pl.pallas_call(
        functools.partial(_attn_body, seq=seq),
        grid=(batch, pairs),
        in_specs=[
            pl.BlockSpec(memory_space=pltpu.SMEM),
            pl.BlockSpec((seq, w), lambda b, p: (b, p)),
            pl.BlockSpec((seq, w), lambda b, p: (b, pairs + p)),
            pl.BlockSpec((seq, w), lambda b, p: (b, 2 * pairs + p)),
            pl.BlockSpec((None, 1, w), lambda b, p: (layer, 0, p)),
            pl.BlockSpec((w, w), lambda b, p: (0, 0)),
        ],
        out_specs=pl.BlockSpec((seq, w), lambda b, p: (b, p)),
        out_shape=jax.ShapeDtypeStruct((m, ATT_WIDTH), bf16),
        scratch_shapes=[
            pltpu.VMEM((nbr, seq, w), f32),
            pltpu.VMEM((nbr, seq, w), f32),
            pltpu.VMEM((nbr, seq, w), f32),
            pltpu.VMEM((nbr, len(ATT_OFFSETS), 2 * ATT_TQ, ATT_TK), f32),
            pltpu.VMEM((3, seq, w), f32),
            pltpu.VMEM((3, ATT_FIN, w), f32),
        ],
        compiler_params=_cparams(2),
        name="dilated_attention",
    )(slopes, qkv, qkv, qkv, attn_norm_row, _group_ones(w, HEAD_DIM))


SSD_L = SSD_CHUNK
SSD_GW = SSD_WIDTH // SSD_GROUPS
SSD_E = SSD_HEADS // SSD_GROUPS
SSD_UA = 8
SSD_UB = 8


def _ssd_body(z_ref, xs_ref, b_ref, c_ref, dt_ref, dtb_ref, alog_ref,
              dsk_ref, ng_ref, expand_ref, o_ref,
              bmt_ref, y_ref, ecb_ref, xinb_ref, decb_ref, st_ref, *, seq):
    n_chunks = seq // SSD_L
    L = SSD_L
    row_i = lax.broadcasted_iota(jnp.int32, (L, L), 0)
    col_i = lax.broadcasted_iota(jnp.int32, (L, L), 1)
    lower = row_i >= col_i
    upper = row_i <= col_i
    lower_b = lower.astype(bf16)
    upper_b = upper.astype(bf16)
    lane8 = lax.broadcasted_iota(jnp.int32, (1, L), 1)
    fwd_lane = lane8 < SSD_E
    a_row = -jnp.exp(alog_ref[...])
    expand = expand_ref[...]
    lane_blk = lax.broadcasted_iota(jnp.int32, (1, SSD_GW), 1) // 64


    def stage1(c):
        rows = pl.ds(pl.multiple_of(c * L, L), L)
        d = dict(c=c, rows=rows, xs=xs_ref[rows, :].astype(f32), bm=b_ref[rows, :], cm=c_ref[rows, :])
        dtv = jax.nn.softplus(dt_ref[rows, :] + dtb_ref[...])
        adt = dtv * a_row
        parts = _split2(adt)
        d["cum"] = jnp.where(fwd_lane, _parts_dot(lower_b, parts), _parts_dot(upper_b, parts))
        d["tot"] = jnp.sum(adt, axis=0, keepdims=True)
        d["dtv"] = dtv
        return d

    def stage2(d):
        cum, tot, dtv = d["cum"], d["tot"], d["dtv"]
        d["src_t"] = (cum - jnp.log(dtv)).T
        d["gram"] = lax.dot_general(d["cm"], d["bm"], _NT, preferred_element_type=f32)
        d["bm_t"] = d["bm"].T
        bmt_ref[d["c"]] = d["bm_t"]
        dec_hi, dec_lo = _split2(jnp.broadcast_to(jnp.exp(tot), (8, L)))
        wide = _dot(jnp.concatenate(
            [jnp.exp(cum).astype(bf16), (jnp.exp(tot - cum) * dtv).astype(bf16), dec_hi, dec_lo], axis=0), expand)
        d["ecum_x"], d["w_x"] = wide[0:L], wide[L:2 * L]
        d["dec"] = wide[2 * L:2 * L + 8] + wide[2 * L + 8:]
        return d

    def stage3(d):
        xs, cum, src_t, gram = d["xs"], d["cum"], d["src_t"], d["gram"]
        m_rows = []
        for e in range(SSD_E):
            df = jnp.exp(jnp.where(lower, cum[:, e:e + 1] - src_t[e:e + 1, :], NEG))
            db = jnp.exp(jnp.where(upper, cum[:, SSD_E + e:SSD_E + e + 1]
                                   - src_t[SSD_E + e:SSD_E + e + 1, :], NEG))
            m_rows.append((gram * (df + db)).astype(bf16))
        y_all = _dot(jnp.concatenate(m_rows, axis=0), xs_ref[d["rows"], :])
        y = dsk_ref[...] * xs
        for e in range(SSD_E):
            y = jnp.where(lane_blk == e, y_all[e * L:(e + 1) * L, :] + y, y)
        d["y"] = y
        d["cs_f"] = _dot(d["bm_t"], (xs * d["w_x"][:, 0:SSD_GW]).astype(bf16))
        xinb_ref[d["rows"], :] = (xs * d["w_x"][:, SSD_GW:]).astype(bf16)
        ecb_ref[d["rows"], :] = d["ecum_x"][:, SSD_GW:]
        decb_ref[d["c"]] = d["dec"][:, SSD_GW:]
        return d

    st_ref[...] = jnp.zeros_like(st_ref)

    def pass_a(i, carry):
        ds = [stage1(i * SSD_UA + u) for u in range(SSD_UA)]
        ds = [stage2(d) for d in ds]
        ds = [stage3(d) for d in ds]
        state = st_ref[...]
        for d in ds:
            y_ref[d["rows"], :] = d["y"] + _dot(d["cm"], state.astype(bf16)) * d["ecum_x"][:, 0:SSD_GW]
            state = state * d["dec"][0:1, 0:SSD_GW] + d["cs_f"]
        st_ref[...] = state
        return carry

    lax.fori_loop(0, n_chunks // SSD_UA, pass_a, 0)

    st_ref[...] = jnp.zeros_like(st_ref)

    def pass_b(i, carry):
        chunks = [n_chunks - 1 - (i * SSD_UB + u) for u in range(SSD_UB)]
        rows = [pl.ds(pl.multiple_of(c * L, L), L) for c in chunks]
        cs = [_dot(bmt_ref[c], xinb_ref[r, :]) for c, r in zip(chunks, rows)]
        state = st_ref[...]
        ys = []
        for c, r, cs_b in zip(chunks, rows, cs):
            ys.append(y_ref[r, :] + _dot(c_ref[r, :], state.astype(bf16)) * ecb_ref[r, :])
            state = state * decb_ref[c][0:1, :] + cs_b
        st_ref[...] = state
        for r, y in zip(rows, ys):
            y = y * _silu(z_ref[r, :].astype(f32))
            ms = jnp.mean(y * y, axis=-1, keepdims=True)
            o_ref[r, :] = (y * lax.rsqrt(ms + EPS) * ng_ref[...]).astype(o_ref.dtype)
        return carry

    lax.fori_loop(0, n_chunks // SSD_UB, pass_b, 0)


def _ssd(layer, pr, dt, dtb_row, alog_row, dskip_row, norm_row, batch, seq):
    m = pr.shape[0]
    expand = np.zeros((SSD_L, 2 * SSD_GW), np.float32)
    for e in range(SSD_E):
        expand[e, 64 * e:64 * (e + 1)] = 1.0
        expand[SSD_E + e, SSD_GW + 64 * e:SSD_GW + 64 * (e + 1)] = 1.0
    expand = jnp.asarray(expand, dtype=bf16)
    return pl.pallas_call(
        functools.partial(_ssd_body, seq=seq),
        grid=(batch, SSD_GROUPS),
        in_specs=[
            pl.BlockSpec((seq, SSD_GW), lambda b, g: (b, g)),
            pl.BlockSpec((seq, SSD_GW), lambda b, g: (b, 2 + g)),
            pl.BlockSpec((seq, SSD_STATE), lambda b, g: (b, 8 + g)),
            pl.BlockSpec((seq, SSD_STATE), lambda b, g: (b, 10 + g)),
            pl.BlockSpec((seq, 128), lambda b, g: (b, g)),
            pl.BlockSpec((None, None, 1, 128), lambda b, g: (layer, g, 0, 0)),
            pl.BlockSpec((None, None, 1, 128), lambda b, g: (layer, g, 0, 0)),
            pl.BlockSpec((None, None, 1, SSD_GW), lambda b, g: (layer, g, 0, 0)),
            pl.BlockSpec((None, None, 1, SSD_GW), lambda b, g: (layer, g, 0, 0)),
            pl.BlockSpec((SSD_L, 2 * SSD_GW), lambda b, g: (0, 0)),
        ],
        out_specs=pl.BlockSpec((seq, SSD_GW), lambda b, g: (b, g)),
        out_shape=jax.ShapeDtypeStruct((m, SSD_WIDTH), bf16),
        scratch_shapes=[
            pltpu.VMEM((seq // SSD_L, SSD_STATE, SSD_L), bf16),
            pltpu.VMEM((seq, SSD_GW), f32),
            pltpu.VMEM((seq, SSD_GW), f32),
            pltpu.VMEM((seq, SSD_GW), bf16),
            pltpu.VMEM((seq // SSD_L, 8, SSD_GW), f32),
            pltpu.VMEM((SSD_STATE, SSD_GW), f32),
        ],
        compiler_params=_cparams(2),
        name="ssd",
    )(pr, pr, pr, pr, dt, dtb_row, alog_row, dskip_row, norm_row, expand)


OUT_TM = 512
OUT_HALO = 16


def _outproj_body(att_ref, ssm_ref, gb_ref, gcp_ref, gc_ref, gcn_ref, hcp_ref, hc_ref, hcn_ref,
                  x_ref, w_ref, cw_ref, cb_ref, ng_ref, gmat_ref, o_ref, stage_ref, *, tiles_per_seq):
    i = pl.program_id(0)
    keep_p = jnp.where(i % tiles_per_seq > 0, 1.0, 0.0)
    keep_n = jnp.where(i % tiles_per_seq < tiles_per_seq - 1, 1.0, 0.0)
    tm = OUT_TM
    h = OUT_HALO
    ready = (_dot(att_ref[...], w_ref[0:ATT_WIDTH, :])
             + _dot(ssm_ref[...], w_ref[ATT_WIDTH:ATT_WIDTH + SSD_WIDTH, :]))
    slabs = []
    for sl in range(CONV_WIDTH // LANES):
        cs = slice(sl * LANES, (sl + 1) * LANES)
        cur = gc_ref[:, cs].astype(f32) * hc_ref[:, cs].astype(f32)
        stage_ref[sl, pl.ds(0, h, stride=2), :] = gcp_ref[:, cs].astype(f32) * hcp_ref[:, cs].astype(f32) * keep_p
        stage_ref[sl, pl.ds(2 * h, tm, stride=2), :] = cur
        stage_ref[sl, pl.ds(2 * (h + tm), h, stride=2), :] = (
            gcn_ref[:, cs].astype(f32) * hcn_ref[:, cs].astype(f32) * keep_n)
        slabs.append(cb_ref[:, cs] + cw_ref[0:1, cs] * stage_ref[sl, pl.ds(2 * (h - 1), tm, stride=2), :]
                     + cw_ref[1:2, cs] * cur
                     + cw_ref[2:3, cs] * stage_ref[sl, pl.ds(2 * (h + 1), tm, stride=2), :])
    y = gb_ref[...].astype(f32) * jnp.concatenate(slabs, axis=-1)
    ms = _dot((y * y).astype(bf16), gmat_ref[...]) * (1.0 / (CONV_WIDTH // CONV_GROUPS))
    sc = (y * lax.rsqrt(ms + EPS) * ng_ref[...]).astype(bf16)
    o_ref[...] = x_ref[...] + ready + _dot(sc, w_ref[ATT_WIDTH + SSD_WIDTH:, :])


def _outproj(layer, att, ssm, pr, x2, w_out, sc_w, sc_b, sc_g, seq):
    m = x2.shape[0]
    tm, h = OUT_TM, OUT_HALO
    per = tm // h
    nb = m // h
    const = lambda i: (0, 0)
    prev = lambda col: (lambda i: (jnp.maximum(i * per - 1, 0), col))
    nxt = lambda col: (lambda i: (jnp.minimum((i + 1) * per, nb - 1), col))
    return pl.pallas_call(
        functools.partial(_outproj_body, tiles_per_seq=seq // tm),
        grid=(m // tm,),
        in_specs=[
            pl.BlockSpec((tm, ATT_WIDTH), lambda i: (i, 0)),
            pl.BlockSpec((tm, SSD_WIDTH), lambda i: (i, 0)),
            pl.BlockSpec((tm, CONV_WIDTH), lambda i: (i, 3)),
            pl.BlockSpec((h, CONV_WIDTH), prev(4)),
            pl.BlockSpec((tm, CONV_WIDTH), lambda i: (i, 4)),
            pl.BlockSpec((h, CONV_WIDTH), nxt(4)),
            pl.BlockSpec((h, CONV_WIDTH), prev(5)),
            pl.BlockSpec((tm, CONV_WIDTH), lambda i: (i, 5)),
            pl.BlockSpec((h, CONV_WIDTH), nxt(5)),
            pl.BlockSpec((tm, D_MODEL), lambda i: (i, 0)),
            _layer_spec(layer, D_MIX, D_MODEL, pipeline_mode=pl.Buffered(1)),
            _layer_spec(layer, 3, CONV_WIDTH),
            _layer_spec(layer, 1, CONV_WIDTH),
            _layer_spec(layer, 1, CONV_WIDTH),
            pl.BlockSpec((CONV_WIDTH, CONV_WIDTH), const),
        ],
        out_specs=pl.BlockSpec((tm, D_MODEL), lambda i: (i, 0)),
        out_shape=jax.ShapeDtypeStruct((m, D_MODEL), f32),
        scratch_shapes=[pltpu.VMEM((CONV_WIDTH // LANES, 2 * (tm + 2 * h), LANES), f32)],
        compiler_params=_cparams(1),
        name="outproj",
    )(att, ssm, pr, pr, pr, pr, pr, pr, pr, x2, w_out, sc_w, sc_b, sc_g,
      _group_ones(CONV_WIDTH, CONV_WIDTH // CONV_GROUPS))


FFN_TM = 512
FFN_HALO = 8
FFN_CHUNK = 256


def _ffn_body(xp_ref, x_ref, xn_ref, g_ref, wup_ref, cw_ref, cb_ref, wdn_ref, fg_ref, o_ref,
              hn_ref, u_ref, act_ref, *, tiles_per_seq, final_norm):
    i = pl.program_id(0)
    tm, h = FFN_TM, FFN_HALO
    keep_p = jnp.where(i % tiles_per_seq > 0, 1.0, 0.0)
    keep_n = jnp.where(i % tiles_per_seq < tiles_per_seq - 1, 1.0, 0.0)
    xall = jnp.concatenate([xp_ref[...] * keep_p, x_ref[...], xn_ref[...] * keep_n], axis=0)
    ms = jnp.mean(xall * xall, axis=-1, keepdims=True)
    hn_ref[...] = ((xall * lax.rsqrt(ms + EPS)) * g_ref[...]).astype(bf16)
    rows = tm + 2 * h
    for c0 in range(0, D_FF, FFN_CHUNK):
        halves = []
        for part in range(2):
            col = part * D_FF + c0
            u = _dot(hn_ref[...], wup_ref[:, col:col + FFN_CHUNK])
            slabs = []
            for sl in range(FFN_CHUNK // LANES):
                u_ref[part, sl, pl.ds(0, rows, stride=2), :] = u[:, sl * LANES:(sl + 1) * LANES]
                cs = slice(col + sl * LANES, col + (sl + 1) * LANES)
                slabs.append(cb_ref[:, cs]
                             + cw_ref[0:1, cs] * u_ref[part, sl, pl.ds(2 * (h - 1), tm, stride=2), :]
                             + cw_ref[1:2, cs] * u[h:h + tm, sl * LANES:(sl + 1) * LANES]
                             + cw_ref[2:3, cs] * u_ref[part, sl, pl.ds(2 * (h + 1), tm, stride=2), :])
            halves.append(jnp.concatenate(slabs, axis=-1))
        act_ref[:, c0:c0 + FFN_CHUNK] = (_silu(halves[0]) * halves[1]).astype(bf16)
    out = x_ref[...] + _dot(act_ref[...], wdn_ref[...])
    if final_norm:
        ms2 = jnp.mean(out * out, axis=-1, keepdims=True)
        out = (out * lax.rsqrt(ms2 + EPS)) * fg_ref[...]
    o_ref[...] = out


def _ffn(layer, x2, g, w_up, conv_w, conv_b, w_down, final_g, seq, final_norm):
    m = x2.shape[0]
    tm, h = FFN_TM, FFN_HALO
    per = tm // h
    nb = m // h
    const = lambda i: (0, 0)
    return pl.pallas_call(
        functools.partial(_ffn_body, tiles_per_seq=seq // tm, final_norm=final_norm),
        grid=(m // tm,),
        in_specs=[
            pl.BlockSpec((h, D_MODEL), lambda i: (jnp.maximum(i * per - 1, 0), 0)),
            pl.BlockSpec((tm, D_MODEL), lambda i: (i, 0)),
            pl.BlockSpec((h, D_MODEL), lambda i: (jnp.minimum((i + 1) * per, nb - 1), 0)),
            _layer_spec(layer, 1, D_MODEL),
            _layer_spec(layer, D_MODEL, 2 * D_FF, pipeline_mode=pl.Buffered(1)),
            _layer_spec(layer, 3, 2 * D_FF),
            _layer_spec(layer, 1, 2 * D_FF),
            _layer_spec(layer, D_FF, D_MODEL, pipeline_mode=pl.Buffered(1)),
            pl.BlockSpec((1, D_MODEL), const),
        ],
        out_specs=pl.BlockSpec((tm, D_MODEL), lambda i: (i, 0)),
        out_shape=jax.ShapeDtypeStruct((m, D_MODEL), f32),
        scratch_shapes=[
            pltpu.VMEM((tm + 2 * h, D_MODEL), bf16),
            pltpu.VMEM((2, FFN_CHUNK // LANES, 2 * (tm + 2 * h), LANES), f32),
            pltpu.VMEM((tm, D_FF), bf16),
        ],
        compiler_params=_cparams(1),
        name="convffn",
    )(x2, x2, x2, g, w_up, conv_w, conv_b, w_down, final_g)


def _prepare(w_in, ssd_dt_bias, ssd_a_log, ssd_d, ssd_norm):
    depth = w_in.shape[0]
    dt0 = 3 * ATT_WIDTH + SSD_WIDTH + SSD_WIDTH + 2 * SSD_GROUPS * SSD_STATE
    w_main = jnp.concatenate([w_in[:, :, :dt0], w_in[:, :, dt0 + 2 * SSD_HEADS:]], axis=2).astype(bf16)
    w_dt_cols = w_in[:, :, dt0:dt0 + 2 * SSD_HEADS]

    def per_group_lanes(v):
        v = v.reshape(depth, 2, SSD_GROUPS, SSD_E).transpose(0, 2, 1, 3).reshape(depth, SSD_GROUPS, 1, 2 * SSD_E)
        return jnp.pad(v, ((0, 0), (0, 0), (0, 0), (0, 128 - 2 * SSD_E)))

    wd = w_dt_cols.reshape(depth, D_MODEL, 2, SSD_GROUPS, SSD_E).transpose(0, 1, 3, 2, 4)
    wd = wd.reshape(depth, D_MODEL, SSD_GROUPS, 2 * SSD_E)
    w_dt = jnp.pad(wd, ((0, 0), (0, 0), (0, 0), (0, 128 - 2 * SSD_E))).reshape(depth, D_MODEL, DT_W).astype(bf16)
    dtb_row = per_group_lanes(ssd_dt_bias)
    alog_row = per_group_lanes(ssd_a_log)
    dskip_row = jnp.repeat(ssd_d, HEAD_DIM, axis=-1).reshape(depth, SSD_GROUPS, 1, SSD_GW)
    norm_row = ssd_norm.reshape(depth, SSD_GROUPS, 1, SSD_GW)
    return w_main, w_dt, dtb_row, alog_row, dskip_row, norm_row


def kernel(x, mix_norm, w_in, ssd_conv_w, ssd_conv_b, ssd_dt_bias, ssd_a_log, ssd_d, ssd_norm, sc_conv_w, sc_conv_b, attn_norm, sc_norm, w_out, ffn_norm, w_up, ffn_conv_w, ffn_conv_b, w_down, final_norm):
    batch, seq, d = x.shape
    depth = w_in.shape[0]
    x2 = x.reshape(batch * seq, d)
    slopes = jnp.asarray(2.0 ** (-8.0 * (np.arange(ATT_HEADS) + 1.0) / ATT_HEADS), dtype=f32)
    w_main, w_dt, dtb_row, alog_row, dskip_row, norm_row = _prepare(w_in, ssd_dt_bias, ssd_a_log, ssd_d, ssd_norm)
    w_out_b, w_up_b, w_down_b = w_out.astype(bf16), w_up.astype(bf16), w_down.astype(bf16)
    rows = lambda a: a[:, None, :]
    for i in range(depth):
        qkv, pr, dt = _inproj(i, x2, rows(mix_norm), w_main, w_dt, ssd_conv_w, rows(ssd_conv_b), seq)
        att = _attention(i, qkv, rows(attn_norm), slopes, batch, seq)
        ssm = _ssd(i, pr, dt, dtb_row, alog_row, dskip_row, norm_row, batch, seq)
        x2 = _outproj(i, att, ssm, pr, x2, w_out_b, sc_conv_w, rows(sc_conv_b), rows(sc_norm), seq)
        x2 = _ffn(i, x2, rows(ffn_norm), w_up_b, ffn_conv_w, rows(ffn_conv_b), w_down_b,
                  final_norm[None, :], seq, final_norm=(i == depth - 1))
    return x2.reshape(batch, seq, d)
```

```python
import functools

import numpy as np
import jax
import jax.numpy as jnp
from jax import lax
from jax.experimental import pallas as pl
from jax.experimental.pallas import tpu as pltpu

f32 = jnp.float32
bf16 = jnp.bfloat16

EPS = 1e-6
NEG = -1e30
LOG2E = 1.4426950408889634

D_MODEL = 1024
HEAD_DIM = 64
ATT_HEADS = 8
ATT_WIDTH = 512
DILATED_BRANCHES = ((128, 1), (512, 4), (2048, 16))
ATT_HALF = 64
SSD_HEADS = 8
SSD_WIDTH = 512
SSD_GROUPS = 2
SSD_STATE = 128
SSD_CONV = 5
SSD_CHUNK = 128
CONV_WIDTH = 512
CONV_GROUPS = 8
D_MIX = 1536
D_FF = 2816
D_IN = 4624

VMEM_LIMIT = 56 * 1024 * 1024
LANES = 128

_NT = (((1,), (1,)), ((), ()))


def _layer_spec(layer, *shape, **kw):
    return pl.BlockSpec((None,) + shape, lambda *_: (layer,) + (0,) * len(shape), **kw)


def _cparams(n_axes):
    return pltpu.CompilerParams(
        dimension_semantics=("arbitrary",) * n_axes, vmem_limit_bytes=VMEM_LIMIT)


def _split2(x):
    hi = x.astype(bf16)
    lo = (x - hi.astype(f32)).astype(bf16)
    return hi, lo


def _dot(a, b):
    return jnp.dot(a, b, preferred_element_type=f32)


def _dot_parts(parts, mat):
    acc = _dot(parts[0], mat)
    for p in parts[1:]:
        acc = acc + _dot(p, mat)
    return acc


def _parts_dot(mat, parts):
    acc = _dot(mat, parts[0])
    for p in parts[1:]:
        acc = acc + _dot(mat, p)
    return acc


def _silu(x):
    return x * jax.nn.sigmoid(x)


def _group_ones(width, group):
    idx = np.arange(width) // group
    return jnp.asarray(idx[:, None] == idx[None, :], dtype=bf16)


IN_TM = 512
IN_CHUNK = 512
QKV_W = 3 * ATT_WIDTH
PR_W = 3072
DT_W = 256
XBC_LO, XBC_HI = 512, 1536
IN_HALO = 8


def _inproj_body(xp_ref, x_ref, xn_ref, g_ref, w_ref, wdt_ref, cw_ref, cb_ref,
                 qkv_ref, pr_ref, dt_ref, hn_ref, stage_ref, *, tiles_per_seq):
    i = pl.program_id(0)
    tm, h = IN_TM, IN_HALO
    keep_p = jnp.where(i % tiles_per_seq > 0, 1.0, 0.0)
    keep_n = jnp.where(i % tiles_per_seq < tiles_per_seq - 1, 1.0, 0.0)
    x = x_ref[...]
    ms = jnp.mean(x * x, axis=-1, keepdims=True)
    hn = ((x * lax.rsqrt(ms + EPS)) * g_ref[...]).astype(bf16)
    halo = jnp.concatenate([xp_ref[...] * keep_p, xn_ref[...] * keep_n], axis=0)
    msh = jnp.mean(halo * halo, axis=-1, keepdims=True)
    hh = ((halo * lax.rsqrt(msh + EPS)) * g_ref[...]).astype(bf16)
    hn_ref[0:2 * h, :] = hh
    hn_ref[2 * h:, :] = hn
    half = SSD_CONV // 2
    for slot, c0 in enumerate(range(XBC_LO, XBC_HI, IN_CHUNK)):
        u = _dot(hn_ref[...], w_ref[:, QKV_W + c0:QKV_W + c0 + IN_CHUNK])
        slabs = []
        for sl in range(IN_CHUNK // LANES):
            ls = slice(sl * LANES, (sl + 1) * LANES)
            cs = slice(c0 - XBC_LO + sl * LANES, c0 - XBC_LO + (sl + 1) * LANES)
            cur = u[2 * h:, ls]
            stage_ref[slot, sl, pl.ds(0, h, stride=2), :] = u[0:h, ls]
            stage_ref[slot, sl, pl.ds(2 * h, tm, stride=2), :] = cur
            stage_ref[slot, sl, pl.ds(2 * (h + tm), h, stride=2), :] = u[h:2 * h, ls]
            acc = cb_ref[:, cs] + cw_ref[half:half + 1, cs] * cur
            for kk in range(SSD_CONV):
                if kk != half:
                    acc = acc + cw_ref[kk:kk + 1, cs] * stage_ref[
                        slot, sl, pl.ds(2 * (h - half + kk), tm, stride=2), :]
            slabs.append(acc)
        pr_ref[:, c0:c0 + IN_CHUNK] = _silu(jnp.concatenate(slabs, axis=-1)).astype(bf16)
    for c0 in range(0, QKV_W, IN_CHUNK):
        qkv_ref[:, c0:c0 + IN_CHUNK] = _dot(hn, w_ref[:, c0:c0 + IN_CHUNK])
    for c0 in list(range(0, XBC_LO, IN_CHUNK)) + list(range(XBC_HI, PR_W, IN_CHUNK)):
        pr_ref[:, c0:c0 + IN_CHUNK] = _dot(
            hn, w_ref[:, QKV_W + c0:QKV_W + c0 + IN_CHUNK]).astype(bf16)
    dt_ref[...] = _dot(hn, wdt_ref[...])


def _inproj(layer, x2, g, w_main, w_dt, conv_w, conv_b, seq):
    m = x2.shape[0]
    tm, h = IN_TM, IN_HALO
    per = tm // h
    nb = m // h
    const = lambda i: (0, 0)
    return pl.pallas_call(
        functools.partial(_inproj_body, tiles_per_seq=seq // tm),
        grid=(m // IN_TM,),
        in_specs=[
            pl.BlockSpec((h, D_MODEL), lambda i: (jnp.maximum(i * per - 1, 0), 0)),
            pl.BlockSpec((IN_TM, D_MODEL), lambda i: (i, 0)),
            pl.BlockSpec((h, D_MODEL), lambda i: (jnp.minimum((i + 1) * per, nb - 1), 0)),
            _layer_spec(layer, 1, D_MODEL),
            _layer_spec(layer, D_MODEL, QKV_W + PR_W, pipeline_mode=pl.Buffered(1)),
            _layer_spec(layer, D_MODEL, DT_W, pipeline_mode=pl.Buffered(1)),
            _layer_spec(layer, SSD_CONV, XBC_HI - XBC_LO),
            _layer_spec(layer, 1, XBC_HI - XBC_LO),
        ],
        out_specs=[
            pl.BlockSpec((IN_TM, QKV_W), lambda i: (i, 0)),
            pl.BlockSpec((IN_TM, PR_W), lambda i: (i, 0)),
            pl.BlockSpec((IN_TM, DT_W), lambda i: (i, 0)),
        ],
        out_shape=[
            jax.ShapeDtypeStruct((m, QKV_W), f32),
            jax.ShapeDtypeStruct((m, PR_W), bf16),
            jax.ShapeDtypeStruct((m, DT_W), f32),
        ],
        scratch_shapes=[
            pltpu.VMEM((tm + 2 * h, D_MODEL), bf16),
            pltpu.VMEM(((XBC_HI - XBC_LO) // IN_CHUNK, IN_CHUNK // LANES, 2 * (tm + 2 * h), LANES), f32),
        ],
        compiler_params=_cparams(1),
        name="inproj",
    )(x2, x2, x2, g, w_main, w_dt, conv_w, conv_b)


ATT_TQ = 128
ATT_TK = ATT_TQ + 2 * ATT_HALF
ATT_FIN = 512
ATT_UNROLL = 32
ATT_OFFSETS = (0, -ATT_HALF, -2 * ATT_HALF)
ATT_PRE = 4


def _attn_body(slopes_ref, q_ref, k_ref, v_ref, g_ref, gmat_ref, o_ref,
               acc_ref, m_ref, l_ref, bias_ref, d4_ref, unperm_ref, *, seq):
    pair = pl.program_id(1)
    lane = lax.broadcasted_iota(jnp.int32, (1, 2 * HEAD_DIM), 1)
    head0 = lane < HEAD_DIM
    qscale = HEAD_DIM ** -0.5 * LOG2E
    qmask0 = jnp.where(head0, qscale, 0.0)
    qmask1 = jnp.where(head0, 0.0, qscale)

    row = lax.broadcasted_iota(jnp.int32, (2 * ATT_TQ, ATT_TK), 0)
    col = lax.broadcasted_iota(jnp.int32, (2 * ATT_TQ, ATT_TK), 1)
    rel0 = col - (row & (ATT_TQ - 1))
    slope_rows = jnp.where(row < ATT_TQ, slopes_ref[2 * pair], slopes_ref[2 * pair + 1])
    for bi, (_, r) in enumerate(DILATED_BRANCHES):
        for vi, off in enumerate(ATT_OFFSETS):
            dist = jnp.abs(rel0 + off)
            bias_ref[bi, vi] = jnp.where(
                dist <= ATT_HALF, dist.astype(f32) * (slope_rows * (-float(r) * LOG2E)), NEG)

    pre_len = seq // ATT_PRE

    def deinterleave(j, carry):
        for a, ref in enumerate((q_ref, k_ref, v_ref)):
            for c in range(ATT_PRE):
                d4_ref[a, pl.ds(c * pre_len + pl.multiple_of(j * ATT_FIN, ATT_FIN), ATT_FIN), :] = (
                    ref[pl.ds(c + ATT_PRE * pl.multiple_of(j * ATT_FIN, ATT_FIN), ATT_FIN, stride=ATT_PRE), :])
        return carry

    lax.fori_loop(0, pre_len // ATT_FIN, deinterleave, 0)

    stages = []
    for bi, (_, r) in enumerate(DILATED_BRANCHES):
        sub_len = seq // r
        tiles_per_class = sub_len // ATT_TQ

        def scores(t, r=r, sub_len=sub_len, tiles_per_class=tiles_per_class):
            cls = t // tiles_per_class
            t0 = (t % tiles_per_class) * ATT_TQ
            k0 = jnp.clip(t0 - ATT_HALF, 0, sub_len - ATT_TK)
            variant = jnp.where(t0 == 0, 0, jnp.where(t0 == sub_len - ATT_TQ, 2, 1))
            if r > ATT_PRE:
                base = (cls % ATT_PRE) * (seq // ATT_PRE) + cls // ATT_PRE
                q_rows = pl.ds(base + (r // ATT_PRE) * t0, ATT_TQ, stride=r // ATT_PRE)
                q = d4_ref[0, q_rows, :]
                k_rows = pl.ds(base + (r // ATT_PRE) * k0, ATT_TK, stride=r // ATT_PRE)
                k = d4_ref[1, k_rows, :]
            else:
                q_rows = pl.ds(cls + r * t0, ATT_TQ, stride=r)
                q = q_ref[q_rows, :]
                k_rows = pl.ds(cls + r * k0, ATT_TK, stride=r)
                k = k_ref[k_rows, :]
            q2 = jnp.concatenate([q * qmask0, q * qmask1], axis=0).astype(bf16)
            s = lax.dot_general(q2, k.astype(bf16), _NT, preferred_element_type=f32)
            return q_rows, k_rows, variant, s

        def softmax(unit, bi=bi):
            q_rows, k_rows, variant, s = unit
            s = s + bias_ref[bi, variant]
            m = jnp.max(s, axis=-1, keepdims=True)
            return q_rows, k_rows, jnp.exp2(s - m).astype(bf16), m

        def values(unit, bi=bi, r=r):
            q_rows, k_rows, p, m = unit
            v = d4_ref[2, k_rows, :] if r > ATT_PRE else v_ref[k_rows, :]
            o0 = _dot(p[0:ATT_TQ], jnp.where(head0, v, 1.0).astype(bf16))
            o1 = _dot(p[ATT_TQ:], jnp.where(head0, 1.0, v).astype(bf16))
            acc_ref[bi, q_rows, :] = jnp.where(head0, o0, o1)
            l_ref[bi, q_rows, :] = jnp.where(head0, o1, o0)
            m_ref[bi, q_rows, :] = jnp.where(head0, m[0:ATT_TQ], m[ATT_TQ:])

        stages.append((scores, softmax, values))

    def tile_group(g, carry):
        units = [(bi, g * ATT_UNROLL + u) for bi in range(len(stages)) for u in range(ATT_UNROLL)]
        st_s, st_p = {}, {}
        for step in range(len(units) + 2):
            if step < len(units):
                bi, t = units[step]
                st_s[step] = stages[bi][0](t)
            if 0 <= step - 1 < len(units):
                st_p[step - 1] = stages[units[step - 1][0]][1](st_s.pop(step - 1))
            if 0 <= step - 2 < len(units):
                stages[units[step - 2][0]][2](st_p.pop(step - 2))
        return carry

    lax.fori_loop(0, seq // (ATT_TQ * ATT_UNROLL), tile_group, 0)

    gmat = gmat_ref[...]

    def combine(t, carry):
        r0 = pl.multiple_of(t * ATT_FIN, ATT_FIN)
        rows = pl.ds(r0, ATT_FIN)
        part = ATT_FIN // ATT_PRE

        def branch_rows(ref, slot, b):
            if DILATED_BRANCHES[b][1] <= ATT_PRE:
                return ref[b, rows, :]
            for c in range(ATT_PRE):
                unperm_ref[slot, pl.ds(c, part, stride=ATT_PRE), :] = (
                    ref[b, pl.ds(c * pre_len + pl.multiple_of(t * part, part), part), :])
            return unperm_ref[slot]

        ms_ = [branch_rows(m_ref, 0, b) for b in range(len(DILATED_BRANCHES))]
        mx = functools.reduce(jnp.maximum, ms_)
        num, den = None, None
        for b, mb in enumerate(ms_):
            e = jnp.exp2(mb - mx)
            nb = e * branch_rows(acc_ref, 1, b)
            db = e * pltpu.roll(branch_rows(l_ref, 2, b), HEAD_DIM, axis=1)
            num = nb if num is None else num + nb
            den = db if den is None else den + db
        y = num / den
        ms = _dot((y * y).astype(bf16), gmat) * (1.0 / HEAD_DIM)
        o_ref[rows, :] = (y * lax.rsqrt(ms + EPS) * g_ref[...]).astype(o_ref.dtype)
        return carry

    lax.fori_loop(0, seq // ATT_FIN, combine, 0)


def _attention(layer, qkv, attn_norm_row, slopes, batch, seq):
    m = qkv.shape[0]
    pairs = ATT_HEADS // 2
    w = 2 * HEAD_DIM
    nbr = len(DILATED_BRANCHES)
    return pl.pallas_call(
        functools.partial(_attn_body, seq=seq),
        grid=(batch, pairs),
        in_specs=[
            pl.BlockSpec(memory_space=pltpu.SMEM),
            pl.BlockSpec((seq, w), lambda b, p: (b, p)),
            pl.BlockSpec((seq, w), lambda b, p: (b, pairs + p)),
            pl.BlockSpec((seq, w), lambda b, p: (b, 2 * pairs + p)),
            pl.BlockSpec((None, 1, w), lambda b, p: (layer, 0, p)),
            pl.BlockSpec((w, w), lambda b, p: (0, 0)),
        ],
        out_specs=pl.BlockSpec((seq, w), lambda b, p: (b, p)),
        out_shape=jax.ShapeDtypeStruct((m, ATT_WIDTH), bf16),
        scratch_shapes=[
            pltpu.VMEM((nbr, seq, w), f32),
            pltpu.VMEM((nbr, seq, w), f32),
            pltpu.VMEM((nbr, seq, w), f32),
            pltpu.VMEM((nbr, len(ATT_OFFSETS), 2 * ATT_TQ, ATT_TK), f32),
            pltpu.VMEM((3, seq, w), f32),
            pltpu.VMEM((3, ATT_FIN, w), f32),
        ],
        compiler_params=_cparams(2),
        name="dilated_attention",
    )(slopes, qkv, qkv, qkv, attn_norm_row, _group_ones(w, HEAD_DIM))


SSD_L = SSD_CHUNK
SSD_GW = SSD_WIDTH // SSD_GROUPS
SSD_E = SSD_HEADS // SSD_GROUPS
SSD_UA = 8
SSD_UB = 8


def _ssd_body(z_ref, xs_ref, b_ref, c_ref, dt_ref, dtb_ref, alog_ref,
              dsk_ref, ng_ref, expand_ref, o_ref,
              bmt_ref, y_ref, ecb_ref, xinb_ref, decb_ref, st_ref, *, seq):
    n_chunks = seq // SSD_L
    L = SSD_L
    row_i = lax.broadcasted_iota(jnp.int32, (L, L), 0)
    col_i = lax.broadcasted_iota(jnp.int32, (L, L), 1)
    lower = row_i >= col_i
    upper = row_i <= col_i
    lower_b = lower.astype(bf16)
    upper_b = upper.astype(bf16)
    lane8 = lax.broadcasted_iota(jnp.int32, (1, L), 1)
    fwd_lane = lane8 < SSD_E
    a_row = -jnp.exp(alog_ref[...])
    expand = expand_ref[...]
    lane_blk = lax.broadcasted_iota(jnp.int32, (1, SSD_GW), 1) // 64


    def stage1(c):
        rows = pl.ds(pl.multiple_of(c * L, L), L)
        d = dict(c=c, rows=rows, xs=xs_ref[rows, :].astype(f32), bm=b_ref[rows, :], cm=c_ref[rows, :])
        dtv = jax.nn.softplus(dt_ref[rows, :] + dtb_ref[...])
        adt = dtv * a_row
        parts = _split2(adt)
        d["cum"] = jnp.where(fwd_lane, _parts_dot(lower_b, parts), _parts_dot(upper_b, parts))
        d["tot"] = jnp.sum(adt, axis=0, keepdims=True)
        d["dtv"] = dtv
        return d

    def stage2(d):
        cum, tot, dtv = d["cum"], d["tot"], d["dtv"]
        d["src_t"] = (cum - jnp.log(dtv)).T
        d["gram"] = lax.dot_general(d["cm"], d["bm"], _NT, preferred_element_type=f32)
        d["bm_t"] = d["bm"].T
        bmt_ref[d["c"]] = d["bm_t"]
        dec_hi, dec_lo = _split2(jnp.broadcast_to(jnp.exp(tot), (8, L)))
        wide = _dot(jnp.concatenate(
            [jnp.exp(cum).astype(bf16), (jnp.exp(tot - cum) * dtv).astype(bf16), dec_hi, dec_lo], axis=0), expand)
        d["ecum_x"], d["w_x"] = wide[0:L], wide[L:2 * L]
        d["dec"] = wide[2 * L:2 * L + 8] + wide[2 * L + 8:]
        return d

    def stage3(d):
        xs, cum, src_t, gram = d["xs"], d["cum"], d["src_t"], d["gram"]
        m_rows = []
        for e in range(SSD_E):
            df = jnp.exp(jnp.where(lower, cum[:, e:e + 1] - src_t[e:e + 1, :], NEG))
            db = jnp.exp(jnp.where(upper, cum[:, SSD_E + e:SSD_E + e + 1]
                                   - src_t[SSD_E + e:SSD_E + e + 1, :], NEG))
            m_rows.append((gram * (df + db)).astype(bf16))
        y_all = _dot(jnp.concatenate(m_rows, axis=0), xs_ref[d["rows"], :])
        y = dsk_ref[...] * xs
        for e in range(SSD_E):
            y = jnp.where(lane_blk == e, y_all[e * L:(e + 1) * L, :] + y, y)
        d["y"] = y
        d["cs_f"] = _dot(d["bm_t"], (xs * d["w_x"][:, 0:SSD_GW]).astype(bf16))
        xinb_ref[d["rows"], :] = (xs * d["w_x"][:, SSD_GW:]).astype(bf16)
        ecb_ref[d["rows"], :] = d["ecum_x"][:, SSD_GW:]
        decb_ref[d["c"]] = d["dec"][:, SSD_GW:]
        return d

    st_ref[...] = jnp.zeros_like(st_ref)

    def pass_a(i, carry):
        ds = [stage1(i * SSD_UA + u) for u in range(SSD_UA)]
        ds = [stage2(d) for d in ds]
        ds = [stage3(d) for d in ds]
        state = st_ref[...]
        for d in ds:
            y_ref[d["rows"], :] = d["y"] + _dot(d["cm"], state.astype(bf16)) * d["ecum_x"][:, 0:SSD_GW]
            state = state * d["dec"][0:1, 0:SSD_GW] + d["cs_f"]
        st_ref[...] = state
        return carry

    lax.fori_loop(0, n_chunks // SSD_UA, pass_a, 0)

    st_ref[...] = jnp.zeros_like(st_ref)

    def pass_b(i, carry):
        chunks = [n_chunks - 1 - (i * SSD_UB + u) for u in range(SSD_UB)]
        rows = [pl.ds(pl.multiple_of(c * L, L), L) for c in chunks]
        cs = [_dot(bmt_ref[c], xinb_ref[r, :]) for c, r in zip(chunks, rows)]
        state = st_ref[...]
        ys = []
        for c, r, cs_b in zip(chunks, rows, cs):
            ys.append(y_ref[r, :] + _dot(c_ref[r, :], state.astype(bf16)) * ecb_ref[r, :])
            state = state * decb_ref[c][0:1, :] + cs_b
        st_ref[...] = state
        for r, y in zip(rows, ys):
            y = y * _silu(z_ref[r, :].astype(f32))
            ms = jnp.mean(y * y, axis=-1, keepdims=True)
            o_ref[r, :] = (y * lax.rsqrt(ms + EPS) * ng_ref[...]).astype(o_ref.dtype)
        return carry

    lax.fori_loop(0, n_chunks // SSD_UB, pass_b, 0)


def _ssd(layer, pr, dt, dtb_row, alog_row, dskip_row, norm_row, batch, seq):
    m = pr.shape[0]
    expand = np.zeros((SSD_L, 2 * SSD_GW), np.float32)
    for e in range(SSD_E):
        expand[e, 64 * e:64 * (e + 1)] = 1.0
        expand[SSD_E + e, SSD_GW + 64 * e:SSD_GW + 64 * (e + 1)] = 1.0
    expand = jnp.asarray(expand, dtype=bf16)
    return pl.pallas_call(
        functools.partial(_ssd_body, seq=seq),
        grid=(batch, SSD_GROUPS),
        in_specs=[
            pl.BlockSpec((seq, SSD_GW), lambda b, g: (b, g)),
            pl.BlockSpec((seq, SSD_GW), lambda b, g: (b, 2 + g)),
            pl.BlockSpec((seq, SSD_STATE), lambda b, g: (b, 8 + g)),
            pl.BlockSpec((seq, SSD_STATE), lambda b, g: (b, 10 + g)),
            pl.BlockSpec((seq, 128), lambda b, g: (b, g)),
            pl.BlockSpec((None, None, 1, 128), lambda b, g: (layer, g, 0, 0)),
            pl.BlockSpec((None, None, 1, 128), lambda b, g: (layer, g, 0, 0)),
            pl.BlockSpec((None, None, 1, SSD_GW), lambda b, g: (layer, g, 0, 0)),
            pl.BlockSpec((None, None, 1, SSD_GW), lambda b, g: (layer, g, 0, 0)),
            pl.BlockSpec((SSD_L, 2 * SSD_GW), lambda b, g: (0, 0)),
        ],
        out_specs=pl.BlockSpec((seq, SSD_GW), lambda b, g: (b, g)),
        out_shape=jax.ShapeDtypeStruct((m, SSD_WIDTH), bf16),
        scratch_shapes=[
            pltpu.VMEM((seq // SSD_L, SSD_STATE, SSD_L), bf16),
            pltpu.VMEM((seq, SSD_GW), f32),
            pltpu.VMEM((seq, SSD_GW), f32),
            pltpu.VMEM((seq, SSD_GW), bf16),
            pltpu.VMEM((seq // SSD_L, 8, SSD_GW), f32),
            pltpu.VMEM((SSD_STATE, SSD_GW), f32),
        ],
        compiler_params=_cparams(2),
        name="ssd",
    )(pr, pr, pr, pr, dt, dtb_row, alog_row, dskip_row, norm_row, expand)


OUT_TM = 1024
OUT_HALO = 16


def _outproj_body(att_ref, ssm_ref, gb_ref, gcp_ref, gc_ref, gcn_ref, hcp_ref, hc_ref, hcn_ref,
                  x_ref, w_ref, cw_ref, cb_ref, ng_ref, gmat_ref, o_ref, stage_ref, *, tiles_per_seq):
    i = pl.program_id(0)
    keep_p = jnp.where(i % tiles_per_seq > 0, 1.0, 0.0)
    keep_n = jnp.where(i % tiles_per_seq < tiles_per_seq - 1, 1.0, 0.0)
    tm = OUT_TM
    h = OUT_HALO
    ready = (_dot(att_ref[...], w_ref[0:ATT_WIDTH, :])
             + _dot(ssm_ref[...], w_ref[ATT_WIDTH:ATT_WIDTH + SSD_WIDTH, :]))
    slabs = []
    for sl in range(CONV_WIDTH // LANES):
        cs = slice(sl * LANES, (sl + 1) * LANES)
        cur = gc_ref[:, cs].astype(f32) * hc_ref[:, cs].astype(f32)
        stage_ref[sl, pl.ds(0, h, stride=2), :] = gcp_ref[:, cs].astype(f32) * hcp_ref[:, cs].astype(f32) * keep_p
        stage_ref[sl, pl.ds(2 * h, tm, stride=2), :] = cur
        stage_ref[sl, pl.ds(2 * (h + tm), h, stride=2), :] = (
            gcn_ref[:, cs].astype(f32) * hcn_ref[:, cs].astype(f32) * keep_n)
        slabs.append(cb_ref[:, cs] + cw_ref[0:1, cs] * stage_ref[sl, pl.ds(2 * (h - 1), tm, stride=2), :]
                     + cw_ref[1:2, cs] * cur
                     + cw_ref[2:3, cs] * stage_ref[sl, pl.ds(2 * (h + 1), tm, stride=2), :])
    y = gb_ref[...].astype(f32) * jnp.concatenate(slabs, axis=-1)
    ms = _dot((y * y).astype(bf16), gmat_ref[...]) * (1.0 / (CONV_WIDTH // CONV_GROUPS))
    sc = (y * lax.rsqrt(ms + EPS) * ng_ref[...]).astype(bf16)
    o_ref[...] = x_ref[...] + ready + _dot(sc, w_ref[ATT_WIDTH + SSD_WIDTH:, :])


def _outproj(layer, att, ssm, pr, x2, w_out, sc_w, sc_b, sc_g, seq):
    m = x2.shape[0]
    tm, h = OUT_TM, OUT_HALO
    per = tm // h
    nb = m // h
    const = lambda i: (0, 0)
    prev = lambda col: (lambda i: (jnp.maximum(i * per - 1, 0), col))
    nxt = lambda col: (lambda i: (jnp.minimum((i + 1) * per, nb - 1), col))
    return pl.pallas_call(
        functools.partial(_outproj_body, tiles_per_seq=seq // tm),
        grid=(m // tm,),
        in_specs=[
            pl.BlockSpec((tm, ATT_WIDTH), lambda i: (i, 0)),
            pl.BlockSpec((tm, SSD_WIDTH), lambda i: (i, 0)),
            pl.BlockSpec((tm, CONV_WIDTH), lambda i: (i, 3)),
            pl.BlockSpec((h, CONV_WIDTH), prev(4)),
            pl.BlockSpec((tm, CONV_WIDTH), lambda i: (i, 4)),
            pl.BlockSpec((h, CONV_WIDTH), nxt(4)),
            pl.BlockSpec((h, CONV_WIDTH), prev(5)),
            pl.BlockSpec((tm, CONV_WIDTH), lambda i: (i, 5)),
            pl.BlockSpec((h, CONV_WIDTH), nxt(5)),
            pl.BlockSpec((tm, D_MODEL), lambda i: (i, 0)),
            _layer_spec(layer, D_MIX, D_MODEL, pipeline_mode=pl.Buffered(1)),
            _layer_spec(layer, 3, CONV_WIDTH),
            _layer_spec(layer, 1, CONV_WIDTH),
            _layer_spec(layer, 1, CONV_WIDTH),
            pl.BlockSpec((CONV_WIDTH, CONV_WIDTH), const),
        ],
        out_specs=pl.BlockSpec((tm, D_MODEL), lambda i: (i, 0)),
        out_shape=jax.ShapeDtypeStruct((m, D_MODEL), f32),
        scratch_shapes=[pltpu.VMEM((CONV_WIDTH // LANES, 2 * (tm + 2 * h), LANES), f32)],
        compiler_params=_cparams(1),
        name="outproj",
    )(att, ssm, pr, pr, pr, pr, pr, pr, pr, x2, w_out, sc_w, sc_b, sc_g,
      _group_ones(CONV_WIDTH, CONV_WIDTH // CONV_GROUPS))


FFN_TM = 512
FFN_HALO = 8
FFN_CHUNK = 256


def _ffn_body(xp_ref, x_ref, xn_ref, g_ref, wup_ref, cw_ref, cb_ref, wdn_ref, fg_ref, o_ref,
              hn_ref, u_ref, act_ref, *, tiles_per_seq, final_norm):
    i = pl.program_id(0)
    tm, h = FFN_TM, FFN_HALO
    keep_p = jnp.where(i % tiles_per_seq > 0, 1.0, 0.0)
    keep_n = jnp.where(i % tiles_per_seq < tiles_per_seq - 1, 1.0, 0.0)
    xall = jnp.concatenate([xp_ref[...] * keep_p, x_ref[...], xn_ref[...] * keep_n], axis=0)
    ms = jnp.mean(xall * xall, axis=-1, keepdims=True)
    hn_ref[...] = ((xall * lax.rsqrt(ms + EPS)) * g_ref[...]).astype(bf16)
    rows = tm + 2 * h
    for c0 in range(0, D_FF, FFN_CHUNK):
        halves = []
        for part in range(2):
            col = part * D_FF + c0
            u = _dot(hn_ref[...], wup_ref[:, col:col + FFN_CHUNK])
            slabs = []
            for sl in range(FFN_CHUNK // LANES):
                u_ref[part, sl, pl.ds(0, rows, stride=2), :] = u[:, sl * LANES:(sl + 1) * LANES]
                cs = slice(col + sl * LANES, col + (sl + 1) * LANES)
                slabs.append(cb_ref[:, cs]
                             + cw_ref[0:1, cs] * u_ref[part, sl, pl.ds(2 * (h - 1), tm, stride=2), :]
                             + cw_ref[1:2, cs] * u[h:h + tm, sl * LANES:(sl + 1) * LANES]
                             + cw_ref[2:3, cs] * u_ref[part, sl, pl.ds(2 * (h + 1), tm, stride=2), :])
            halves.append(jnp.concatenate(slabs, axis=-1))
        act_ref[:, c0:c0 + FFN_CHUNK] = (_silu(halves[0]) * halves[1]).astype(bf16)
    out = x_ref[...] + _dot(act_ref[...], wdn_ref[...])
    if final_norm:
        ms2 = jnp.mean(out * out, axis=-1, keepdims=True)
        out = (out * lax.rsqrt(ms2 + EPS)) * fg_ref[...]
    o_ref[...] = out


def _ffn(layer, x2, g, w_up, conv_w, conv_b, w_down, final_g, seq, final_norm):
    m = x2.shape[0]
    tm, h = FFN_TM, FFN_HALO
    per = tm // h
    nb = m // h
    const = lambda i: (0, 0)
    return pl.pallas_call(
        functools.partial(_ffn_body, tiles_per_seq=seq // tm, final_norm=final_norm),
        grid=(m // tm,),
        in_specs=[
            pl.BlockSpec((h, D_MODEL), lambda i: (jnp.maximum(i * per - 1, 0), 0)),
            pl.BlockSpec((tm, D_MODEL), lambda i: (i, 0)),
            pl.BlockSpec((h, D_MODEL), lambda i: (jnp.minimum((i + 1) * per, nb - 1), 0)),
            _layer_spec(layer, 1, D_MODEL),
            _layer_spec(layer, D_MODEL, 2 * D_FF, pipeline_mode=pl.Buffered(1)),
            _layer_spec(layer, 3, 2 * D_FF),
            _layer_spec(layer, 1, 2 * D_FF),
            _layer_spec(layer, D_FF, D_MODEL, pipeline_mode=pl.Buffered(1)),
            pl.BlockSpec((1, D_MODEL), const),
        ],
        out_specs=pl.BlockSpec((tm, D_MODEL), lambda i: (i, 0)),
        out_shape=jax.ShapeDtypeStruct((m, D_MODEL), f32),
        scratch_shapes=[
            pltpu.VMEM((tm + 2 * h, D_MODEL), bf16),
            pltpu.VMEM((2, FFN_CHUNK // LANES, 2 * (tm + 2 * h), LANES), f32),
            pltpu.VMEM((tm, D_FF), bf16),
        ],
        compiler_params=_cparams(1),
        name="convffn",
    )(x2, x2, x2, g, w_up, conv_w, conv_b, w_down, final_g)


def _prepare(w_in, ssd_dt_bias, ssd_a_log, ssd_d, ssd_norm):
    depth = w_in.shape[0]
    dt0 = 3 * ATT_WIDTH + SSD_WIDTH + SSD_WIDTH + 2 * SSD_GROUPS * SSD_STATE
    w_main = jnp.concatenate([w_in[:, :, :dt0], w_in[:, :, dt0 + 2 * SSD_HEADS:]], axis=2).astype(bf16)
    w_dt_cols = w_in[:, :, dt0:dt0 + 2 * SSD_HEADS]

    def per_group_lanes(v):
        v = v.reshape(depth, 2, SSD_GROUPS, SSD_E).transpose(0, 2, 1, 3).reshape(depth, SSD_GROUPS, 1, 2 * SSD_E)
        return jnp.pad(v, ((0, 0), (0, 0), (0, 0), (0, 128 - 2 * SSD_E)))

    wd = w_dt_cols.reshape(depth, D_MODEL, 2, SSD_GROUPS, SSD_E).transpose(0, 1, 3, 2, 4)
    wd = wd.reshape(depth, D_MODEL, SSD_GROUPS, 2 * SSD_E)
    w_dt = jnp.pad(wd, ((0, 0), (0, 0), (0, 0), (0, 128 - 2 * SSD_E))).reshape(depth, D_MODEL, DT_W).astype(bf16)
    dtb_row = per_group_lanes(ssd_dt_bias)
    alog_row = per_group_lanes(ssd_a_log)
    dskip_row = jnp.repeat(ssd_d, HEAD_DIM, axis=-1).reshape(depth, SSD_GROUPS, 1, SSD_GW)
    norm_row = ssd_norm.reshape(depth, SSD_GROUPS, 1, SSD_GW)
    return w_main, w_dt, dtb_row, alog_row, dskip_row, norm_row


def kernel(x, mix_norm, w_in, ssd_conv_w, ssd_conv_b, ssd_dt_bias, ssd_a_log, ssd_d, ssd_norm, sc_conv_w, sc_conv_b, attn_norm, sc_norm, w_out, ffn_norm, w_up, ffn_conv_w, ffn_conv_b, w_down, final_norm):
    batch, seq, d = x.shape
    depth = w_in.shape[0]
    x2 = x.reshape(batch * seq, d)
    slopes = jnp.asarray(2.0 ** (-8.0 * (np.arange(ATT_HEADS) + 1.0) / ATT_HEADS), dtype=f32)
    w_main, w_dt, dtb_row, alog_row, dskip_row, norm_row = _prepare(w_in, ssd_dt_bias, ssd_a_log, ssd_d, ssd_norm)
    w_out_b, w_up_b, w_down_b = w_out.astype(bf16), w_up.astype(bf16), w_down.astype(bf16)
    rows = lambda a: a[:, None, :]
    for i in range(depth):
        qkv, pr, dt = _inproj(i, x2, rows(mix_norm), w_main, w_dt, ssd_conv_w, rows(ssd_conv_b), seq)
        att = _attention(i, qkv, rows(attn_norm), slopes, batch, seq)
        ssm = _ssd(i, pr, dt, dtb_row, alog_row, dskip_row, norm_row, batch, seq)
        x2 = _outproj(i, att, ssm, pr, x2, w_out_b, sc_conv_w, rows(sc_conv_b), rows(sc_norm), seq)
        x2 = _ffn(i, x2, rows(ffn_norm), w_up_b, ffn_conv_w, rows(ffn_conv_b), w_down_b,
                  final_norm[None, :], seq, final_norm=(i == depth - 1))
    return x2.reshape(batch, seq, d)
```
